```python
import functools
import jax, jax.numpy as jnp
from jax import lax
import numpy as np

D_MODEL = 2048
BATCH = 2
SEQ = 4096
DEPTH = 1
DEC_BATCH = 128
DEC_SEQ = 4
PAST_LEN = 2048
PAGE_SIZE = 128

SB_HEADS = 8
SB_HEAD_DIM = 128
SB_WIDTH = SB_HEADS * SB_HEAD_DIM
SB_BLOCK = 128
SB_BIAS_INIT = -6.0
RW_HEADS = 16
RW_HEAD_DIM = 64
RW_WIDTH = RW_HEADS * RW_HEAD_DIM
DECAY_LORA = 64
ICLR_LORA = 64
GATE_LORA = 160
RW_PROJ = 3 * RW_WIDTH + DECAY_LORA + ICLR_LORA + GATE_LORA
IN_WIDTH = 3 * SB_WIDTH + RW_PROJ + 2 * D_MODEL
FFN_DIM = 4 * D_MODEL
RMS_EPS = 1e-6
GN_EPS = 64e-5

kernel_name = "stickbreak_rwkv7_gated_hybrid_step"


def rms_norm(x, g):
    xf = x.astype(jnp.float32)
    y = xf * lax.rsqrt(jnp.mean(xf * xf, axis=-1, keepdims=True) + RMS_EPS)
    return (y * g.astype(jnp.float32)).astype(x.dtype)


def sb_weights(z, q_pos):
    causal = jnp.arange(z.shape[-1])[None, :] < q_pos[:, None]
    log_1m = jnp.where(causal, jax.nn.log_sigmoid(-z), 0.0)
    between = lax.cumsum(log_1m, axis=3, reverse=True) - log_1m
    return jnp.where(causal, jnp.exp(jax.nn.log_sigmoid(z) + between), 0.0)


def sb_scores(q, k, bias):
    z = jnp.einsum("bqhd,bkhd->bhqk", q, k, preferred_element_type=jnp.float32) * (SB_HEAD_DIM ** -0.5)
    return z + bias.astype(jnp.float32)[None, :, None, None]


def sb_prompt(q, k, v, bias):
    b, t, h, d = q.shape
    nb = t // SB_BLOCK
    q_blocks = jnp.moveaxis(q.reshape(b, nb, SB_BLOCK, h, d), 1, 0)

    def one_block(args):
        q_blk, blk = args
        q_pos = blk * SB_BLOCK + jnp.arange(SB_BLOCK)
        w = sb_weights(sb_scores(q_blk, k, bias), q_pos)
        return jnp.einsum("bhqk,bkhd->bqhd", w.astype(v.dtype), v)

    out = lax.map(one_block, (q_blocks, jnp.arange(nb)))
    return jnp.moveaxis(out, 0, 1).reshape(b, t, h, d)


def sb_sample(q, k, v, bias, past_k, past_v):
    p = past_k.shape[1]
    past_k = past_k.astype(k.dtype)
    past_v = past_v.astype(v.dtype)
    z = jnp.concatenate([sb_scores(q, past_k, bias), sb_scores(q, k, bias)], axis=-1)
    w = sb_weights(z, p + jnp.arange(q.shape[1])).astype(v.dtype)
    return (jnp.einsum("bhqk,bkhd->bqhd", w[..., :p], past_v)
            + jnp.einsum("bhqk,bkhd->bqhd", w[..., p:], v))


def wkv7_scan(r, decay, k, v, kk, a, s0):
    def step(s, inp):
        r_t, w_t, k_t, v_t, kk_t, a_t = inp
        sa = jnp.einsum("bhij,bhj->bhi", s, -kk_t)
        s = (s * w_t[:, :, None, :] + sa[..., None] * (kk_t * a_t)[:, :, None, :]
             + v_t[..., None] * k_t[:, :, None, :])
        return s, jnp.einsum("bhij,bhj->bhi", s, r_t)

    xs = tuple(jnp.moveaxis(z.astype(jnp.float32), 1, 0) for z in (r, decay, k, v, kk, a))
    s, y = lax.scan(step, s0.astype(jnp.float32), xs)
    return jnp.moveaxis(y, 0, 1), s


def rwkv7_mixer(cols, shift_prev, wkv_prev, mu_rw, w0, w2, a0, a2, g2, k_k, k_a, r_k, lnx_w, lnx_b):
    b, t, _ = cols.shape
    prev = jnp.concatenate([shift_prev[:, None].astype(cols.dtype), cols[:, :-1]], axis=1)
    mixed = cols + (prev - cols) * mu_rw
    splits = [RW_WIDTH, 2 * RW_WIDTH, 3 * RW_WIDTH, 3 * RW_WIDTH + DECAY_LORA,
              3 * RW_WIDTH + DECAY_LORA + ICLR_LORA]
    r, k, v, wd, ad, gd = jnp.split(mixed, splits, axis=-1)
    w_log = -jax.nn.softplus(-(w0 + jnp.tanh(wd) @ w2)) - 0.5
    decay = jnp.exp(-jnp.exp(w_log.astype(jnp.float32)))
    a = jax.nn.sigmoid(a0 + ad @ a2)
    g = jax.nn.sigmoid(gd) @ g2

    def heads(z):
        return z.reshape(b, t, RW_HEADS, RW_HEAD_DIM)

    hshape = (RW_HEADS, RW_HEAD_DIM)
    kk = heads(k * k_k).astype(jnp.float32)
    kk = kk / jnp.maximum(jnp.sqrt(jnp.sum(kk * kk, axis=-1, keepdims=True)), 1e-12)
    r, v, decay, a = heads(r), heads(v), heads(decay), heads(a)
    k = heads(k) * (1.0 + (a - 1.0) * k_a.reshape(hshape))
    y, s = wkv7_scan(r, decay, k, v, kk, a, wkv_prev)
    mu = jnp.mean(y, axis=-1, keepdims=True)
    var = jnp.mean(jnp.square(y - mu), axis=-1, keepdims=True)
    yn = (y - mu) * lax.rsqrt(var + GN_EPS)
    yn = yn * lnx_w.reshape(hshape).astype(jnp.float32) + lnx_b.reshape(hshape).astype(jnp.float32)
    bonus = jnp.sum(r.astype(jnp.float32) * k.astype(jnp.float32) * r_k.astype(jnp.float32),
                    axis=-1, keepdims=True) * v.astype(jnp.float32)
    out = (yn + bonus).reshape(b, t, RW_WIDTH).astype(cols.dtype) * g
    return out, s.astype(wkv_prev.dtype), cols[:, -1]


def hybrid_layer(x, attend, shift_prev, wkv_prev, ln1_g, w_in, sb_bias, mu_rw, w0, w2, a0, a2, g2,
                 k_k, k_a, r_k, lnx_w, lnx_b, w_br_sb, w_br_rw, w_out, ln2_g, w_up, w_down):
    b, t, _ = x.shape
    h = rms_norm(x, ln1_g)
    proj = h @ w_in
    q, k, v, rw_cols, gates = jnp.split(
        proj, [SB_WIDTH, 2 * SB_WIDTH, 3 * SB_WIDTH, 3 * SB_WIDTH + RW_PROJ], axis=-1)
    q = q.reshape(b, t, SB_HEADS, SB_HEAD_DIM)
    k = k.reshape(b, t, SB_HEADS, SB_HEAD_DIM)
    v = v.reshape(b, t, SB_HEADS, SB_HEAD_DIM)
    y_sb = attend(q, k, v, sb_bias).reshape(b, t, SB_WIDTH)
    y_rw, wkv_new, shift_new = rwkv7_mixer(rw_cols, shift_prev, wkv_prev, mu_rw, w0, w2, a0, a2,
                                           g2, k_k, k_a, r_k, lnx_w, lnx_b)
    gate_sb, gate_rw = jnp.split(gates, 2, axis=-1)
    merged = (jax.nn.sigmoid(gate_sb) * (y_sb @ w_br_sb)
              + jax.nn.sigmoid(gate_rw) * (y_rw @ w_br_rw))
    x = x + merged @ w_out
    h2 = rms_norm(x, ln2_g)
    x = x + jnp.square(jax.nn.relu(h2 @ w_up)) @ w_down
    return x, k, v, wkv_new, shift_new


def setup_inputs(seed: int = 0) -> dict:
    key = jax.random.key(seed)
    ks = jax.random.split(key, 32)
    n_pages = PAST_LEN // PAGE_SIZE
    n_pool = (DEC_BATCH * n_pages * 5) // 4

    def nrm(k, shape, s):
        return s * jax.random.normal(k, shape, jnp.float32)

    perm = jax.random.permutation(ks[6], n_pool)
    page_table = perm[:DEC_BATCH * n_pages].reshape(DEC_BATCH, n_pages).astype(jnp.int32)
    kv_shape = (DEPTH, n_pool, PAGE_SIZE, SB_HEADS, SB_HEAD_DIM)
    L = DEPTH
    return {
        "x_prompt": nrm(ks[0], (BATCH, SEQ, D_MODEL), 1.0),
        "x_sample": nrm(ks[1], (DEC_BATCH, DEC_SEQ, D_MODEL), 1.0),
        "cache_k": nrm(ks[2], kv_shape, 1.0),
        "cache_v": nrm(ks[3], kv_shape, 1.0),
        "state_wkv": nrm(ks[4], (DEPTH, DEC_BATCH, RW_HEADS, RW_HEAD_DIM, RW_HEAD_DIM), 0.5),
        "state_shift": nrm(ks[5], (DEPTH, DEC_BATCH, RW_PROJ), 1.0),
        "page_table": page_table,
        "ln1_g": 1.0 + nrm(ks[7], (L, D_MODEL), 0.01),
        "w_in": nrm(ks[8], (L, D_MODEL, IN_WIDTH), D_MODEL ** -0.5),
        "sb_bias": SB_BIAS_INIT + nrm(ks[27], (L, SB_HEADS), 0.1),
        "mu_rw": jax.random.uniform(ks[9], (L, RW_PROJ), jnp.float32),
        "w0": jax.random.uniform(ks[10], (L, RW_WIDTH), jnp.float32, -6.5, -1.5),
        "w2": nrm(ks[11], (L, DECAY_LORA, RW_WIDTH), 0.5 * DECAY_LORA ** -0.5),
        "a0": nrm(ks[12], (L, RW_WIDTH), 0.1),
        "a2": nrm(ks[13], (L, ICLR_LORA, RW_WIDTH), 0.5 * ICLR_LORA ** -0.5),
        "g2": nrm(ks[14], (L, GATE_LORA, RW_WIDTH), GATE_LORA ** -0.5),
        "k_k": 0.85 + nrm(ks[15], (L, RW_WIDTH), 0.02),
        "k_a": 1.0 + nrm(ks[16], (L, RW_WIDTH), 0.02),
        "r_k": nrm(ks[17], (L, RW_HEADS, RW_HEAD_DIM), 0.1),
        "lnx_w": 1.0 + nrm(ks[18], (L, RW_WIDTH), 0.01),
        "lnx_b": nrm(ks[19], (L, RW_WIDTH), 0.01),
        "w_br_sb": nrm(ks[20], (L, SB_WIDTH, D_MODEL), SB_WIDTH ** -0.5),
        "w_br_rw": nrm(ks[21], (L, RW_WIDTH, D_MODEL), RW_WIDTH ** -0.5),
        "w_out": nrm(ks[22], (L, D_MODEL, D_MODEL), D_MODEL ** -0.5),
        "ln2_g": 1.0 + nrm(ks[23], (L, D_MODEL), 0.01),
        "w_up": nrm(ks[24], (L, D_MODEL, FFN_DIM), D_MODEL ** -0.5),
        "w_down": nrm(ks[25], (L, FFN_DIM, D_MODEL), FFN_DIM ** -0.5),
        "lnf_g": 1.0 + nrm(ks[26], (D_MODEL,), 0.01),
    }


def reference(x_prompt, x_sample, cache_k, cache_v, state_wkv, state_shift, page_table,
              ln1_g, w_in, sb_bias, mu_rw, w0, w2, a0, a2, g2, k_k, k_a, r_k, lnx_w, lnx_b,
              w_br_sb, w_br_rw, w_out, ln2_g, w_up, w_down, lnf_g):
    n_pages = PAST_LEN // PAGE_SIZE
    bp = x_prompt.shape[0]
    bs = x_sample.shape[0]
    xp, xs = x_prompt, x_sample
    kp_l, vp_l, wp_l, sp_l, ks_l, vs_l, ws_l, ss_l = [], [], [], [], [], [], [], []
    for l in range(DEPTH):
        lw = (ln1_g[l], w_in[l], sb_bias[l], mu_rw[l], w0[l], w2[l], a0[l], a2[l], g2[l], k_k[l],
              k_a[l], r_k[l], lnx_w[l], lnx_b[l], w_br_sb[l], w_br_rw[l], w_out[l], ln2_g[l],
              w_up[l], w_down[l])
        wkv0 = jnp.zeros((bp, RW_HEADS, RW_HEAD_DIM, RW_HEAD_DIM), state_wkv.dtype)
        shift0 = jnp.zeros((bp, RW_PROJ), state_shift.dtype)
        xp, kp, vp, wp, sp = hybrid_layer(xp, sb_prompt, shift0, wkv0, *lw)
        past_k = cache_k[l][page_table].reshape(bs, n_pages * PAGE_SIZE, SB_HEADS, SB_HEAD_DIM)
        past_v = cache_v[l][page_table].reshape(bs, n_pages * PAGE_SIZE, SB_HEADS, SB_HEAD_DIM)
        attend = functools.partial(sb_sample, past_k=past_k, past_v=past_v)
        xs, kn, vn, wn, sn = hybrid_layer(xs, attend, state_shift[l], state_wkv[l], *lw)
        kp_l.append(kp); vp_l.append(vp); wp_l.append(wp); sp_l.append(sp)
        ks_l.append(kn); vs_l.append(vn); ws_l.append(wn); ss_l.append(sn)
    y_prompt = rms_norm(xp, lnf_g)
    y_sample = rms_norm(xs, lnf_g)
    return (y_prompt, y_sample,
            jnp.stack(kp_l), jnp.stack(vp_l), jnp.stack(wp_l), jnp.stack(sp_l),
            jnp.stack(ks_l), jnp.stack(vs_l), jnp.stack(ws_l), jnp.stack(ss_l))
```

```python
import functools

import jax
import jax.numpy as jnp
from jax import lax
from jax.experimental import pallas as pl
from jax.experimental.pallas import tpu as pltpu

F32 = jnp.float32
BF16 = jnp.bfloat16
HIGHEST = lax.Precision.HIGHEST

LANES = 128
SB_HEADS = 8
SB_HEAD_DIM = 128
SB_WIDTH = SB_HEADS * SB_HEAD_DIM
RW_HEADS = 16
RW_HEAD_DIM = 64
RW_WIDTH = RW_HEADS * RW_HEAD_DIM
RW_PAIRS = RW_WIDTH // LANES
DECAY_LORA = 64
ICLR_LORA = 64
GATE_LORA = 160
RW_PROJ = 3 * RW_WIDTH + DECAY_LORA + ICLR_LORA + GATE_LORA
RMS_EPS = 1e-6
GN_EPS = 64e-5
PAGE_SIZE = 128
VMEM_LIMIT = 48 * 1024 * 1024


def _nt(a, b, precision=None):
    return lax.dot_general(a, b, (((1,), (1,)), ((), ())), precision=precision,
                           preferred_element_type=F32)


def _tn(a, b, precision=None):
    return lax.dot_general(a, b, (((0,), (0,)), ((), ())), precision=precision,
                           preferred_element_type=F32)


def _softplus(z):
    return jnp.maximum(z, 0.0) + jnp.log1p(jnp.exp(-jnp.abs(z)))


def _rms_kernel(x_ref, g_ref, o_ref):
    x = x_ref[...]
    y = x * lax.rsqrt(jnp.mean(x * x, axis=-1, keepdims=True) + RMS_EPS)
    o_ref[...] = (y * g_ref[...]).astype(o_ref.dtype)


def rms_norm(x, g, out_dtype, tm=256):
    m, d = x.shape
    return pl.pallas_call(
        _rms_kernel,
        grid=(m // tm,),
        in_specs=[pl.BlockSpec((tm, d), lambda i: (i, 0)),
                  pl.BlockSpec((1, d), lambda i: (0, 0))],
        out_specs=pl.BlockSpec((tm, d), lambda i: (i, 0)),
        out_shape=jax.ShapeDtypeStruct((m, d), out_dtype),
        compiler_params=pltpu.CompilerParams(dimension_semantics=("parallel",)),
        name="rms_norm",
    )(x, g.reshape(1, d))


def _mm_kernel(*refs, nk, n_extra, epilogue):
    a_ref, b_ref = refs[0], refs[1]
    extras = refs[2:2 + n_extra]
    o_ref = refs[2 + n_extra]

    def finish(acc):
        if epilogue is not None:
            acc = epilogue(acc, *[e[...] for e in extras])
        o_ref[...] = acc.astype(o_ref.dtype)

    if nk == 1:
        finish(jnp.dot(a_ref[...], b_ref[...], preferred_element_type=F32))
        return
    acc_ref = refs[3 + n_extra]
    k = pl.program_id(2)

    @pl.when(k == 0)
    def _():
        acc_ref[...] = jnp.zeros_like(acc_ref)

    acc_ref[...] += jnp.dot(a_ref[...], b_ref[...], preferred_element_type=F32)

    @pl.when(k == nk - 1)
    def _():
        finish(acc_ref[...])


def matmul(a, b, out_dtype, *, tm, tn, tk=None, extras=(), epilogue=None, name="matmul"):
    m, kdim = a.shape
    _, n = b.shape
    tk = kdim if tk is None else tk
    nk = kdim // tk
    assert m % tm == 0 and n % tn == 0 and kdim % tk == 0
    in_specs = [pl.BlockSpec((tm, tk), lambda i, j, k: (i, k)),
                pl.BlockSpec((tk, tn), lambda i, j, k: (k, j))]
    in_specs += [pl.BlockSpec((tm, tn), lambda i, j, k: (i, j)) for _ in extras]
    scratch = [pltpu.VMEM((tm, tn), F32)] if nk > 1 else []
    return pl.pallas_call(
        functools.partial(_mm_kernel, nk=nk, n_extra=len(extras), epilogue=epilogue),
        grid=(m // tm, n // tn, nk),
        in_specs=in_specs,
        out_specs=pl.BlockSpec((tm, tn), lambda i, j, k: (i, j)),
        out_shape=jax.ShapeDtypeStruct((m, n), out_dtype),
        scratch_shapes=scratch,
        compiler_params=pltpu.CompilerParams(
            dimension_semantics=("parallel", "parallel", "arbitrary"),
            vmem_limit_bytes=VMEM_LIMIT),
        name=name,
    )(a, b, *extras)


def _sb_block(z, causal, tri, carry, vb):
    l1m = -_softplus(z)
    if causal is not None:
        l1m = jnp.where(causal, l1m, 0.0)
    hi = l1m.astype(BF16)
    lo = (l1m - hi.astype(F32)).astype(BF16)
    between = (jnp.dot(hi, tri, preferred_element_type=F32)
               + jnp.dot(lo, tri, preferred_element_type=F32))
    w = jnp.exp(z + l1m + between + carry)
    if causal is not None:
        w = jnp.where(causal, w, 0.0)
    out = jnp.dot(w.astype(BF16), vb, preferred_element_type=F32)
    return out, jnp.sum(l1m, axis=1, keepdims=True)


def _strict_upper_ones(n):
    row = lax.broadcasted_iota(jnp.int32, (n, n), 0)
    col = lax.broadcasted_iota(jnp.int32, (n, n), 1)
    return (row > col).astype(BF16)


def _sb_prompt_kernel(bias_ref, q_ref, k_ref, v_ref, o_ref, *, tq, scale):
    h = pl.program_id(1)
    qi = pl.program_id(2)
    bias = bias_ref[h]
    q = q_ref[...].astype(BF16)
    row = lax.broadcasted_iota(jnp.int32, (tq, tq), 0)
    col = lax.broadcasted_iota(jnp.int32, (tq, tq), 1)
    tri = (row > col).astype(BF16)

    def body(jj, state):
        acc, carry = state
        j = qi - jj
        start = pl.multiple_of(j * tq, tq)
        kb = k_ref[pl.ds(start, tq), :].astype(BF16)
        vb = v_ref[pl.ds(start, tq), :].astype(BF16)
        z = _nt(q, kb) * scale + bias
        causal = (col + j * tq) < (row + qi * tq)
        out, rowsum = _sb_block(z, causal, tri, carry, vb)
        return acc + out, carry + rowsum

    acc, _ = lax.fori_loop(
        0, qi + 1, body,
        (jnp.zeros((tq, SB_HEAD_DIM), F32), jnp.zeros((tq, 1), F32)))
    o_ref[...] = acc


def sb_prompt_attention(qkv, sb_bias, batch, seq, tq=128):
    nq = seq // tq
    kernel = functools.partial(_sb_prompt_kernel, tq=tq, scale=SB_HEAD_DIM ** -0.5)
    return pl.pallas_call(
        kernel,
        grid_spec=pltpu.PrefetchScalarGridSpec(
            num_scalar_prefetch=0,
            grid=(batch, SB_HEADS, nq),
            in_specs=[
                pl.BlockSpec(memory_space=pltpu.SMEM),
                pl.BlockSpec((tq, SB_HEAD_DIM), lambda b, h, i: (b * nq + i, h)),
                pl.BlockSpec((seq, SB_HEAD_DIM), lambda b, h, i: (b, SB_HEADS + h)),
                pl.BlockSpec((seq, SB_HEAD_DIM), lambda b, h, i: (b, 2 * SB_HEADS + h)),
            ],
            out_specs=pl.BlockSpec((tq, SB_HEAD_DIM), lambda b, h, i: (b * nq + i, h)),
        ),
        out_shape=jax.ShapeDtypeStruct((batch * seq, SB_WIDTH), F32),
        compiler_params=pltpu.CompilerParams(
            dimension_semantics=("parallel", "parallel", "arbitrary"),
            vmem_limit_bytes=VMEM_LIMIT),
        name="sb_prompt_attention",
    )(sb_bias, qkv, qkv, qkv)


def _sb_sample_kernel(pt_ref, q_ref, bias_ref, kc_ref, vc_ref, kn_ref, vn_ref, o_ref,
                      acc_ref, carry_ref, kpad_ref, vpad_ref, *, n_pages, n_new, scale):
    del pt_ref
    s = pl.program_id(1)
    rows = acc_ref.shape[0]
    tri = _strict_upper_ones(PAGE_SIZE)
    q = q_ref[0]

    def block(kb, vb, causal):
        z = _nt(q, kb) * scale + bias_ref[...]
        out, rowsum = _sb_block(z, causal, tri, carry_ref[...], vb)
        acc_ref[...] += out
        carry_ref[...] += rowsum

    @pl.when(s == 0)
    def _():
        acc_ref[...] = jnp.zeros_like(acc_ref)
        carry_ref[...] = jnp.zeros_like(carry_ref)
        kpad_ref[...] = jnp.zeros_like(kpad_ref)
        vpad_ref[...] = jnp.zeros_like(vpad_ref)
        kpad_ref[0:n_new, :] = kn_ref[0]
        vpad_ref[0:n_new, :] = vn_ref[0]
        row = lax.broadcasted_iota(jnp.int32, (rows, PAGE_SIZE), 0)
        col = lax.broadcasted_iota(jnp.int32, (rows, PAGE_SIZE), 1)
        causal = col < row // SB_HEADS
        block(kpad_ref[...].astype(BF16), vpad_ref[...].astype(BF16), causal)

    @pl.when(s > 0)
    def _():
        block(kc_ref[0].astype(BF16), vc_ref[0].astype(BF16), None)

    @pl.when(s == n_pages)
    def _():
        hrow = lax.broadcasted_iota(jnp.int32, (SB_HEADS, SB_WIDTH), 0)
        hcol = lax.broadcasted_iota(jnp.int32, (SB_HEADS, SB_WIDTH), 1) // SB_HEAD_DIM
        own = (hrow == hcol).astype(F32)
        for t in range(n_new):
            blk = acc_ref[t * SB_HEADS:(t + 1) * SB_HEADS, :] * own
            o_ref[0, t:t + 1, :] = jnp.sum(blk, axis=0, keepdims=True)


def sb_sample_attention(q, k_new, v_new, cache_k, cache_v, page_table, sb_bias):
    seqs, n_new, _ = q.shape
    n_pages = page_table.shape[1]
    rows = n_new * SB_HEADS
    q4 = q.reshape(seqs, n_new, SB_HEADS, 1, SB_HEAD_DIM)
    eye = jnp.eye(SB_HEADS, dtype=F32).reshape(1, 1, SB_HEADS, SB_HEADS, 1)
    qbd = (q4 * eye).reshape(seqs, rows, SB_WIDTH).astype(BF16)
    bias = jnp.broadcast_to(jnp.tile(sb_bias, n_new)[:, None], (rows, PAGE_SIZE)).astype(F32)

    def page_map(b, s, pt):
        return (pt[b, n_pages - jnp.maximum(s, 1)], 0, 0)

    kernel = functools.partial(_sb_sample_kernel, n_pages=n_pages, n_new=n_new,
                               scale=SB_HEAD_DIM ** -0.5)
    return pl.pallas_call(
        kernel,
        grid_spec=pltpu.PrefetchScalarGridSpec(
            num_scalar_prefetch=1,
            grid=(seqs, n_pages + 1),
            in_specs=[
                pl.BlockSpec((1, rows, SB_WIDTH), lambda b, s, pt: (b, 0, 0)),
                pl.BlockSpec((rows, PAGE_SIZE), lambda b, s, pt: (0, 0)),
                pl.BlockSpec((1, PAGE_SIZE, SB_WIDTH), page_map),
                pl.BlockSpec((1, PAGE_SIZE, SB_WIDTH), page_map),
                pl.BlockSpec((1, n_new, SB_WIDTH), lambda b, s, pt: (b, 0, 0)),
                pl.BlockSpec((1, n_new, SB_WIDTH), lambda b, s, pt: (b, 0, 0)),
            ],
            out_specs=pl.BlockSpec((1, n_new, SB_WIDTH), lambda b, s, pt: (b, 0, 0)),
            scratch_shapes=[
                pltpu.VMEM((rows, SB_WIDTH), F32),
                pltpu.VMEM((rows, 1), F32),
                pltpu.VMEM((PAGE_SIZE, SB_WIDTH), F32),
                pltpu.VMEM((PAGE_SIZE, SB_WIDTH), F32),
            ],
        ),
        out_shape=jax.ShapeDtypeStruct((seqs, n_new, SB_WIDTH), F32),
        compiler_params=pltpu.CompilerParams(
            dimension_semantics=("parallel", "arbitrary"),
            vmem_limit_bytes=VMEM_LIMIT),
        name="sb_sample_attention",
    )(page_table, qbd, bias, cache_k, cache_v, k_new, v_new)


def _rwkv_kernel(r_ref, k_ref, v_ref, lw_ref, la_ref, g_ref, prm_ref, s0_ref, y_ref, st_ref,
                 state_ref, *, chunk, t_valid, n_chunks):
    c = pl.program_id(2)

    @pl.when(c == 0)
    def _():
        state_ref[...] = s0_ref[0, 0]

    r = r_ref[0]
    k = k_ref[0]
    v = v_ref[0]
    w0, a0, k_k, k_a, r_k, lnx_w, lnx_b = (prm_ref[i:i + 1, :] for i in range(7))

    lane = lax.broadcasted_iota(jnp.int32, (1, LANES), 1)
    head_masks = ((lane < RW_HEAD_DIM).astype(F32), (lane >= RW_HEAD_DIM).astype(F32))
    ri = lax.broadcasted_iota(jnp.int32, (LANES, LANES), 0)
    ci = lax.broadcasted_iota(jnp.int32, (LANES, LANES), 1)
    same_head = ((ri // RW_HEAD_DIM) == (ci // RW_HEAD_DIM)).astype(F32)
    tr = lax.broadcasted_iota(jnp.int32, (chunk, chunk), 0)
    tc = lax.broadcasted_iota(jnp.int32, (chunk, chunk), 1)
    strict = tr > tc
    incl = tr >= tc

    def dot(a, b):
        return jnp.dot(a, b, precision=HIGHEST, preferred_element_type=F32)

    w_log = -_softplus(-(w0 + lw_ref[0])) - 0.5
    log_decay = -jnp.exp(w_log)
    if t_valid < chunk:
        trow = lax.broadcasted_iota(jnp.int32, (chunk, 1), 0)
        log_decay = jnp.where(trow < t_valid, log_decay, 0.0)
    a = jax.nn.sigmoid(a0 + la_ref[0])
    kk = k * k_k
    kk = kk / jnp.maximum(jnp.sqrt(dot(kk * kk, same_head)), 1e-12)
    k2 = k * (1.0 + (a - 1.0) * k_a)
    b = kk * a

    cs = dot(incl.astype(F32), log_decay)
    cs_last = cs[chunk - 1:chunk, :]
    at = -kk * jnp.exp(cs - log_decay)
    ginv = jnp.exp(-cs)
    bt = b * ginv
    kt = k2 * ginv
    rt = r * jnp.exp(cs)
    to_end = jnp.exp(cs_last - cs)
    t0 = state_ref[...]

    eye_c = (tr == tc).astype(F32)
    rhs = dot(at, t0)
    y = dot(rt, t0)
    a_ab, a_rb = [], []
    for m in head_masks:
        ah = at * m
        rh = rt * m
        vh = v * m
        a_ab.append(jnp.where(strict, _nt(ah, bt, HIGHEST), 0.0))
        a_rb.append(jnp.where(incl, _nt(rh, bt, HIGHEST), 0.0))
        rhs = rhs + dot(jnp.where(strict, _nt(ah, kt, HIGHEST), 0.0), vh)
        y = y + dot(jnp.where(incl, _nt(rh, kt, HIGHEST), 0.0), vh)

    u = jnp.zeros_like(rhs)
    for m, low in zip(head_masks, a_ab):
        inv = eye_c + low
        power = low
        span = 2
        while span < chunk:
            power = dot(power, power)
            inv = inv + dot(inv, power)
            span *= 2
        u = u + dot(inv, rhs * m)
    for m, arb in zip(head_masks, a_rb):
        y = y + dot(arb, u * m)

    decay_diag = jnp.where(ri == ci, jnp.broadcast_to(jnp.exp(cs_last), (LANES, LANES)), 0.0)
    state_ref[...] = dot(decay_diag, t0) + same_head * (
        _tn(b * to_end, u, HIGHEST) + _tn(k2 * to_end, v, HIGHEST))

    inv_n = 1.0 / RW_HEAD_DIM
    mu = dot(y, same_head) * inv_n
    d = y - mu
    var = dot(d * d, same_head) * inv_n
    yn = d * lax.rsqrt(var + GN_EPS) * lnx_w + lnx_b
    bonus = dot(r * k2 * r_k, same_head) * v
    y_ref[0] = (yn + bonus) * g_ref[0]

    @pl.when(c == n_chunks - 1)
    def _():
        st_ref[0, 0] = state_ref[...]


def rwkv7_scan(r, k, v, lora_w, lora_a, g, params, state_t, chunk, t_valid):
    batch, t, _ = r.shape
    n_chunks = t // chunk
    seq_spec = pl.BlockSpec((1, chunk, LANES), lambda b, p, c: (b, c, p))
    st_spec = pl.BlockSpec((1, 1, LANES, LANES), lambda b, p, c: (b, p, 0, 0))
    kernel = functools.partial(_rwkv_kernel, chunk=chunk, t_valid=t_valid, n_chunks=n_chunks)
    return pl.pallas_call(
        kernel,
        grid=(batch, RW_PAIRS, n_chunks),
        in_specs=[seq_spec] * 6 + [pl.BlockSpec((8, LANES), lambda b, p, c: (0, p)), st_spec],
        out_specs=[seq_spec, st_spec],
        out_shape=[jax.ShapeDtypeStruct((batch, t, RW_WIDTH), F32),
                   jax.ShapeDtypeStruct((batch, RW_PAIRS, LANES, LANES), F32)],
        scratch_shapes=[pltpu.VMEM((LANES, LANES), F32)],
        compiler_params=pltpu.CompilerParams(
            dimension_semantics=("parallel", "parallel", "arbitrary"),
            vmem_limit_bytes=VMEM_LIMIT),
        name="rwkv7_scan",
    )(r, k, v, lora_w, lora_a, g, params, state_t)


def _pack_state(s):
    batch = s.shape[0]
    st = jnp.swapaxes(s, -1, -2).reshape(batch, RW_PAIRS, 2, RW_HEAD_DIM, RW_HEAD_DIM)
    z = jnp.zeros_like(st[:, :, 0])
    top = jnp.concatenate([st[:, :, 0], z], axis=-1)
    bot = jnp.concatenate([z, st[:, :, 1]], axis=-1)
    return jnp.concatenate([top, bot], axis=-2)


def _unpack_state(t):
    batch = t.shape[0]
    d = RW_HEAD_DIM
    pair = jnp.stack([t[:, :, :d, :d], t[:, :, d:, d:]], axis=2)
    return jnp.swapaxes(pair.reshape(batch, RW_HEADS, d, d), -1, -2)


def rwkv7_branch(cols, shift_prev, wkv_prev, mu_rw, params, w2, a2, g2, chunk, t_pad):
    batch, t, _ = cols.shape
    prev = jnp.concatenate([shift_prev[:, None], cols[:, :-1]], axis=1)
    mixed = cols + (prev - cols) * mu_rw
    w = RW_WIDTH
    r, k, v = mixed[..., :w], mixed[..., w:2 * w], mixed[..., 2 * w:3 * w]
    wd = mixed[..., 3 * w:3 * w + DECAY_LORA]
    ad = mixed[..., 3 * w + DECAY_LORA:3 * w + DECAY_LORA + ICLR_LORA]
    gd = mixed[..., 3 * w + DECAY_LORA + ICLR_LORA:]
    m = batch * t
    tm = min(512, m)
    lora_w = matmul(jnp.tanh(wd).reshape(m, -1).astype(BF16), w2, F32, tm=tm, tn=512, name="lora_w")
    lora_a = matmul(ad.reshape(m, -1).astype(BF16), a2, F32, tm=tm, tn=512, name="lora_a")
    g = matmul(jax.nn.sigmoid(gd).reshape(m, -1).astype(BF16), g2, F32, tm=tm, tn=512, name="lora_g")

    def seq(z):
        z = z.reshape(batch, t, w)
        if t_pad > t:
            z = jnp.pad(z, ((0, 0), (0, t_pad - t), (0, 0)))
        return z

    y, st = rwkv7_scan(seq(r), seq(k), seq(v), seq(lora_w), seq(lora_a), seq(g), params,
                       _pack_state(wkv_prev), chunk, min(t, chunk))
    return y[:, :t].reshape(m, w), _unpack_state(st)


def kernel(x_prompt, x_sample, cache_k, cache_v, state_wkv, state_shift, page_table, ln1_g, w_in, sb_bias, mu_rw, w0, w2, a0, a2, g2, k_k, k_a, r_k, lnx_w, lnx_b, w_br_sb, w_br_rw, w_out, ln2_g, w_up, w_down, lnf_g):
    bp, seq, d_model = x_prompt.shape
    bs, dec, _ = x_sample.shape
    mp, ms = bp * seq, bs * dec
    n_pool = cache_k.shape[1]
    tm = 512

    x = jnp.concatenate([x_prompt.reshape(mp, d_model), x_sample.reshape(ms, d_model)], axis=0)
    h = rms_norm(x, ln1_g[0], BF16)

    w_in0 = w_in[0]
    qkv_w = w_in0[:, :3 * SB_WIDTH].astype(BF16)
    rw_cols_pad = 3584
    rw_w = jnp.pad(w_in0[:, 3 * SB_WIDTH:3 * SB_WIDTH + RW_PROJ],
                   ((0, 0), (0, rw_cols_pad - RW_PROJ))).astype(BF16)
    gate_w = w_in0[:, 3 * SB_WIDTH + RW_PROJ:].astype(BF16)

    qkv = matmul(h, qkv_w, F32, tm=tm, tn=512, name="proj_qkv")
    rw_cols = matmul(h, rw_w, F32, tm=tm, tn=512, name="proj_rw")[:, :RW_PROJ]
    gates = matmul(h, gate_w, F32, tm=tm, tn=512, name="proj_gates")

    y_sb_p = sb_prompt_attention(qkv, sb_bias[0], bp, seq)
    qkv_s = qkv[mp:].reshape(bs, dec, 3 * SB_WIDTH)
    q_s, k_s, v_s = (qkv_s[..., i * SB_WIDTH:(i + 1) * SB_WIDTH] for i in range(3))
    y_sb_s = sb_sample_attention(
        q_s, k_s, v_s,
        cache_k[0].reshape(n_pool, PAGE_SIZE, SB_WIDTH),
        cache_v[0].reshape(n_pool, PAGE_SIZE, SB_WIDTH),
        page_table, sb_bias[0])
    y_sb = jnp.concatenate([y_sb_p, y_sb_s.reshape(ms, SB_WIDTH)], axis=0).astype(BF16)

    params = jnp.stack([w0[0], a0[0], k_k[0], k_a[0], r_k[0].reshape(-1), lnx_w[0], lnx_b[0],
                        jnp.zeros((RW_WIDTH,), F32)])
    w2b, a2b, g2b = w2[0].astype(BF16), a2[0].astype(BF16), g2[0].astype(BF16)
    cols_p = rw_cols[:mp].reshape(bp, seq, RW_PROJ)
    cols_s = rw_cols[mp:].reshape(bs, dec, RW_PROJ)
    y_rw_p, wkv_p = rwkv7_branch(
        cols_p, jnp.zeros((bp, RW_PROJ), F32),
        jnp.zeros((bp, RW_HEADS, RW_HEAD_DIM, RW_HEAD_DIM), F32),
        mu_rw[0], params, w2b, a2b, g2b, chunk=128, t_pad=seq)
    y_rw_s, wkv_s = rwkv7_branch(
        cols_s, state_shift[0], state_wkv[0], mu_rw[0], params, w2b, a2b, g2b, chunk=8, t_pad=8)
    y_rw = jnp.concatenate([y_rw_p, y_rw_s], axis=0).astype(BF16)

    gate_sb, gate_rw = gates[:, :d_model], gates[:, d_model:]
    m_sb = matmul(y_sb, w_br_sb[0].astype(BF16), F32, tm=tm, tn=512, extras=(gate_sb,),
                  epilogue=lambda acc, gt: jax.nn.sigmoid(gt) * acc, name="branch_sb")
    merged = matmul(y_rw, w_br_rw[0].astype(BF16), BF16, tm=tm, tn=512, extras=(gate_rw, m_sb),
                    epilogue=lambda acc, gt, prev: prev + jax.nn.sigmoid(gt) * acc,
                    name="branch_rw")
    x1 = matmul(merged, w_out[0].astype(BF16), F32, tm=tm, tn=512, extras=(x,),
                epilogue=lambda acc, res: res + acc, name="out_proj")
    h2 = rms_norm(x1, ln2_g[0], BF16)
    up = matmul(h2, w_up[0].astype(BF16), BF16, tm=tm, tn=1024,
                epilogue=lambda acc: jnp.square(jnp.maximum(acc, 0.0)), name="mlp_up")
    x2 = matmul(up, w_down[0].astype(BF16), F32, tm=tm, tn=512, tk=2048, extras=(x1,),
                epilogue=lambda acc, res: res + acc, name="mlp_down")
    y = rms_norm(x2, lnf_g, F32)

    def kv_out(rows, lo, b, t):
        return rows[:, lo:lo + SB_WIDTH].reshape(1, b, t, SB_HEADS, SB_HEAD_DIM)

    return (y[:mp].reshape(bp, seq, d_model),
            y[mp:].reshape(bs, dec, d_model),
            kv_out(qkv[:mp], SB_WIDTH, bp, seq),
            kv_out(qkv[:mp], 2 * SB_WIDTH, bp, seq),
            wkv_p[None],
            cols_p[:, -1][None],
            kv_out(qkv[mp:], SB_WIDTH, bs, dec),
            kv_out(qkv[mp:], 2 * SB_WIDTH, bs, dec),
            wkv_s[None],
            cols_s[:, -1][None])
```

```python
import functools

import jax
import jax.numpy as jnp
from jax import lax
from jax.experimental import pallas as pl
from jax.experimental.pallas import tpu as pltpu

F32 = jnp.float32
BF16 = jnp.bfloat16
HIGHEST = lax.Precision.HIGHEST

LANES = 128
SB_HEADS = 8
SB_HEAD_DIM = 128
SB_WIDTH = SB_HEADS * SB_HEAD_DIM
RW_HEADS = 16
RW_HEAD_DIM = 64
RW_WIDTH = RW_HEADS * RW_HEAD_DIM
RW_PAIRS = RW_WIDTH // LANES
DECAY_LORA = 64
ICLR_LORA = 64
GATE_LORA = 160
RW_PROJ = 3 * RW_WIDTH + DECAY_LORA + ICLR_LORA + GATE_LORA
RMS_EPS = 1e-6
GN_EPS = 64e-5
PAGE_SIZE = 128
VMEM_LIMIT = 48 * 1024 * 1024


def _nt(a, b, precision=None):
    return lax.dot_general(a, b, (((1,), (1,)), ((), ())), precision=precision,
                           preferred_element_type=F32)


def _tn(a, b, precision=None):
    return lax.dot_general(a, b, (((0,), (0,)), ((), ())), precision=precision,
                           preferred_element_type=F32)


def _softplus(z):
    return jnp.maximum(z, 0.0) + jnp.log1p(jnp.exp(-jnp.abs(z)))


def _rms_kernel(x_ref, g_ref, o_ref):
    x = x_ref[...]
    y = x * lax.rsqrt(jnp.mean(x * x, axis=-1, keepdims=True) + RMS_EPS)
    o_ref[...] = (y * g_ref[...]).astype(o_ref.dtype)


def rms_norm(x, g, out_dtype, tm=256):
    m, d = x.shape
    return pl.pallas_call(
        _rms_kernel,
        grid=(m // tm,),
        in_specs=[pl.BlockSpec((tm, d), lambda i: (i, 0)),
                  pl.BlockSpec((1, d), lambda i: (0, 0))],
        out_specs=pl.BlockSpec((tm, d), lambda i: (i, 0)),
        out_shape=jax.ShapeDtypeStruct((m, d), out_dtype),
        compiler_params=pltpu.CompilerParams(dimension_semantics=("parallel",)),
        name="rms_norm",
    )(x, g.reshape(1, d))


def _mm_kernel(*refs, nk, n_extra, epilogue):
    a_ref, b_ref = refs[0], refs[1]
    extras = refs[2:2 + n_extra]
    o_ref = refs[2 + n_extra]

    def finish(acc):
        if epilogue is not None:
            acc = epilogue(acc, *[e[...] for e in extras])
        o_ref[...] = acc.astype(o_ref.dtype)

    if nk == 1:
        finish(jnp.dot(a_ref[...], b_ref[...], preferred_element_type=F32))
        return
    acc_ref = refs[3 + n_extra]
    k = pl.program_id(2)

    @pl.when(k == 0)
    def _():
        acc_ref[...] = jnp.zeros_like(acc_ref)

    acc_ref[...] += jnp.dot(a_ref[...], b_ref[...], preferred_element_type=F32)

    @pl.when(k == nk - 1)
    def _():
        finish(acc_ref[...])


def matmul(a, b, out_dtype, *, tm, tn, tk=None, extras=(), epilogue=None, name="matmul"):
    m, kdim = a.shape
    _, n = b.shape
    tk = kdim if tk is None else tk
    nk = kdim // tk
    assert m % tm == 0 and n % tn == 0 and kdim % tk == 0
    in_specs = [pl.BlockSpec((tm, tk), lambda i, j, k: (i, k)),
                pl.BlockSpec((tk, tn), lambda i, j, k: (k, j))]
    in_specs += [pl.BlockSpec((tm, tn), lambda i, j, k: (i, j)) for _ in extras]
    scratch = [pltpu.VMEM((tm, tn), F32)] if nk > 1 else []
    return pl.pallas_call(
        functools.partial(_mm_kernel, nk=nk, n_extra=len(extras), epilogue=epilogue),
        grid=(m // tm, n // tn, nk),
        in_specs=in_specs,
        out_specs=pl.BlockSpec((tm, tn), lambda i, j, k: (i, j)),
        out_shape=jax.ShapeDtypeStruct((m, n), out_dtype),
        scratch_shapes=scratch,
        compiler_params=pltpu.CompilerParams(
            dimension_semantics=("parallel", "parallel", "arbitrary"),
            vmem_limit_bytes=VMEM_LIMIT),
        name=name,
    )(a, b, *extras)


def _split(x):
    hi = x.astype(BF16)
    lo = (x - hi.astype(F32)).astype(BF16)
    return hi, lo


def _sb_weights(z, causal, tri, carry):
    n = z.shape[0]
    l1m = -_softplus(z)
    if causal is not None:
        l1m = jnp.where(causal, l1m, 0.0)
    hi, lo = _split(l1m)
    both = jnp.dot(jnp.concatenate([hi, lo], axis=0), tri, preferred_element_type=F32)
    between = both[:n] + both[n:]
    w = jnp.exp(z + l1m + between + carry)
    if causal is not None:
        w = jnp.where(causal, w, 0.0)
    return w.astype(BF16), jnp.sum(l1m, axis=1, keepdims=True)


def _strict_upper_ones(n):
    row = lax.broadcasted_iota(jnp.int32, (n, n), 0)
    col = lax.broadcasted_iota(jnp.int32, (n, n), 1)
    return (row > col).astype(BF16)


def _sb_prompt_kernel(bias_ref, q_ref, k_ref, v_ref, o_ref, *carry_refs, tq, tk, heads, scale):
    grp = pl.program_id(1)
    qi = pl.program_id(2)
    d = SB_HEAD_DIM
    row = lax.broadcasted_iota(jnp.int32, (tq, tk), 0)
    col = lax.broadcasted_iota(jnp.int32, (tq, tk), 1)
    tri = _strict_upper_ones(tk)
    qs = [q_ref[:, h * d:(h + 1) * d].astype(BF16) for h in range(heads)]
    biases = [bias_ref[grp * heads + h] for h in range(heads)]
    o_ref[...] = jnp.zeros_like(o_ref)
    for c_ref in carry_refs:
        c_ref[...] = jnp.zeros_like(c_ref)
    n_blocks = (qi + 1) * (tq // tk)

    def body(jj, _):
        j = n_blocks - 1 - jj
        start = pl.multiple_of(j * tk, tk)
        causal = (col + j * tk) < (row + qi * tq)
        for h in range(heads):
            hs = slice(h * d, (h + 1) * d)
            kb = k_ref[pl.ds(start, tk), hs].astype(BF16)
            vb = v_ref[pl.ds(start, tk), hs].astype(BF16)
            z = _nt(qs[h], kb) * scale + biases[h]
            w, rowsum = _sb_weights(z, causal, tri, carry_refs[h][...])
            o_ref[:, hs] += jnp.dot(w, vb, preferred_element_type=F32)
            carry_refs[h][...] += rowsum
        return 0

    lax.fori_loop(0, n_blocks, body, 0)


def sb_prompt_attention(qkv, sb_bias, batch, seq, tq=256, tk=128, heads=2):
    nq = seq // tq
    groups = SB_HEADS // heads
    width = heads * SB_HEAD_DIM
    kernel = functools.partial(_sb_prompt_kernel, tq=tq, tk=tk, heads=heads,
                               scale=SB_HEAD_DIM ** -0.5)
    return pl.pallas_call(
        kernel,
        grid=(batch, groups, nq),
        in_specs=[
            pl.BlockSpec(memory_space=pltpu.SMEM),
            pl.BlockSpec((tq, width), lambda b, g, i: (b * nq + i, g)),
            pl.BlockSpec((seq, width), lambda b, g, i: (b, groups + g)),
            pl.BlockSpec((seq, width), lambda b, g, i: (b, 2 * groups + g)),
        ],
        out_specs=pl.BlockSpec((tq, width), lambda b, g, i: (b * nq + i, g)),
        out_shape=jax.ShapeDtypeStruct((batch * seq, SB_WIDTH), F32),
        scratch_shapes=[pltpu.VMEM((tq, 1), F32) for _ in range(heads)],
        compiler_params=pltpu.CompilerParams(
            dimension_semantics=("parallel", "parallel", "arbitrary"),
            vmem_limit_bytes=VMEM_LIMIT),
        name="sb_prompt_attention",
    )(sb_bias, qkv, qkv, qkv)


SAMPLE_ROWS = 8


def _sb_sample_kernel(pt_ref, q_ref, bias_ref, ka_ref, va_ref, kb_ref, vb_ref, kn_ref, vn_ref,
                      o_ref, acc_ref, carry_ref, kpad_ref, vpad_ref, *, n_steps, n_new, scale):
    del pt_ref
    s = pl.program_id(1)
    d = SB_HEAD_DIM
    rr = SAMPLE_ROWS
    tri = _strict_upper_ones(PAGE_SIZE)
    q = q_ref[0]

    def scores(k_heads):
        z = jnp.concatenate(
            [_nt(q[h * rr:(h + 1) * rr], k_heads[h]) for h in range(SB_HEADS)], axis=0)
        return z * scale + bias_ref[...]

    def values(w, v_heads):
        return jnp.concatenate(
            [jnp.dot(w[h * rr:(h + 1) * rr], v_heads[h], preferred_element_type=F32)
             for h in range(SB_HEADS)], axis=0)

    def page_heads(ref):
        return [ref[:, h, :].astype(BF16) for h in range(SB_HEADS)]

    def pad_heads(ref):
        return [ref[:, h * d:(h + 1) * d].astype(BF16) for h in range(SB_HEADS)]

    @pl.when(s == 0)
    def _():
        kpad_ref[...] = jnp.zeros_like(kpad_ref)
        vpad_ref[...] = jnp.zeros_like(vpad_ref)
        kpad_ref[0:n_new, :] = kn_ref[0]
        vpad_ref[0:n_new, :] = vn_ref[0]
        row = lax.broadcasted_iota(jnp.int32, (SB_HEADS * rr, PAGE_SIZE), 0)
        col = lax.broadcasted_iota(jnp.int32, (SB_HEADS * rr, PAGE_SIZE), 1)
        causal = col < row % rr
        w, rowsum = _sb_weights(scores(pad_heads(kpad_ref)), causal, tri, 0.0)
        acc_ref[...] = values(w, pad_heads(vpad_ref))
        carry_ref[...] = rowsum

    carry = carry_ref[...]
    wa, rsa = _sb_weights(scores(page_heads(ka_ref)), None, tri, carry)
    wb, rsb = _sb_weights(scores(page_heads(kb_ref)), None, tri, carry + rsa)
    acc_ref[...] += values(wa, page_heads(va_ref)) + values(wb, page_heads(vb_ref))
    carry_ref[...] = carry + rsa + rsb

    @pl.when(s == n_steps - 1)
    def _():
        for h in range(SB_HEADS):
            o_ref[0, :, h * d:(h + 1) * d] = acc_ref[h * rr:h * rr + n_new, :]


def sb_sample_attention(q, k_new, v_new, cache_k, cache_v, page_table, sb_bias):
    seqs, n_new, _ = q.shape
    n_pages = page_table.shape[1]
    assert n_pages % 2 == 0 and n_new <= SAMPLE_ROWS
    n_steps = n_pages // 2
    rows = SB_HEADS * SAMPLE_ROWS
    qh = jnp.swapaxes(q.reshape(seqs, n_new, SB_HEADS, SB_HEAD_DIM), 1, 2)
    qh = jnp.pad(qh, ((0, 0), (0, 0), (0, SAMPLE_ROWS - n_new), (0, 0)))
    qh = qh.reshape(seqs, rows, SB_HEAD_DIM).astype(BF16)
    bias = jnp.broadcast_to(jnp.repeat(sb_bias, SAMPLE_ROWS)[:, None], (rows, PAGE_SIZE)).astype(F32)

    def page_spec(offset):
        return pl.BlockSpec(
            (None, None, PAGE_SIZE, SB_HEADS, SB_HEAD_DIM),
            lambda b, s, pt: (0, pt[b, n_pages - 1 - offset - 2 * s], 0, 0, 0))

    new_spec = pl.BlockSpec((1, n_new, SB_WIDTH), lambda b, s, pt: (b, 0, 0))
    kernel = functools.partial(_sb_sample_kernel, n_steps=n_steps, n_new=n_new,
                               scale=SB_HEAD_DIM ** -0.5)
    return pl.pallas_call(
        kernel,
        grid_spec=pltpu.PrefetchScalarGridSpec(
            num_scalar_prefetch=1,
            grid=(seqs, n_steps),
            in_specs=[
                pl.BlockSpec((1, rows, SB_HEAD_DIM), lambda b, s, pt: (b, 0, 0)),
                pl.BlockSpec((rows, PAGE_SIZE), lambda b, s, pt: (0, 0)),
                page_spec(0), page_spec(0), page_spec(1), page_spec(1),
                new_spec, new_spec,
            ],
            out_specs=new_spec,
            scratch_shapes=[
                pltpu.VMEM((rows, SB_HEAD_DIM), F32),
                pltpu.VMEM((rows, 1), F32),
                pltpu.VMEM((PAGE_SIZE, SB_WIDTH), F32),
                pltpu.VMEM((PAGE_SIZE, SB_WIDTH), F32),
            ],
        ),
        out_shape=jax.ShapeDtypeStruct((seqs, n_new, SB_WIDTH), F32),
        compiler_params=pltpu.CompilerParams(
            dimension_semantics=("parallel", "arbitrary"),
            vmem_limit_bytes=VMEM_LIMIT),
        name="sb_sample_attention",
    )(page_table, qh, bias, cache_k, cache_v, cache_k, cache_v, k_new, v_new)


_NN = (((1,), (0,)), ((), ()))
_NT = (((1,), (1,)), ((), ()))
_TN = (((0,), (0,)), ((), ()))


def _dot3(a, b, dims=_NN):
    ah, al = _split(a)
    bh, bl = _split(b)

    def d(x, y):
        return lax.dot_general(x, y, dims, preferred_element_type=F32)

    return d(ah, bh) + (d(al, bh) + d(ah, bl))


def _dot_ones_rhs(a, ones):
    n = a.shape[0]
    ah, al = _split(a)
    both = jnp.dot(jnp.concatenate([ah, al], axis=0), ones, preferred_element_type=F32)
    return both[:n] + both[n:]


def _dot_ones_lhs(ones, b):
    n = b.shape[1]
    bh, bl = _split(b)
    both = jnp.dot(ones, jnp.concatenate([bh, bl], axis=1), preferred_element_type=F32)
    return both[:, :n] + both[:, n:]


def _rwkv_kernel(r_ref, k_ref, v_ref, lw_ref, la_ref, g_ref, prm_ref, s0_ref, y_ref, st_ref,
                 state_ref, *, chunk, t_valid, n_chunks):
    c = pl.program_id(2)

    @pl.when(c == 0)
    def _():
        state_ref[...] = s0_ref[0, 0]

    r = r_ref[0]
    k = k_ref[0]
    v = v_ref[0]
    w0, a0, k_k, k_a, r_k, lnx_w, lnx_b = (prm_ref[i:i + 1, :] for i in range(7))

    lane = lax.broadcasted_iota(jnp.int32, (1, LANES), 1)
    head_masks = ((lane < RW_HEAD_DIM).astype(F32), (lane >= RW_HEAD_DIM).astype(F32))
    ri = lax.broadcasted_iota(jnp.int32, (LANES, LANES), 0)
    ci = lax.broadcasted_iota(jnp.int32, (LANES, LANES), 1)
    same_head = (ri // RW_HEAD_DIM) == (ci // RW_HEAD_DIM)
    same_head_ones = same_head.astype(BF16)
    tr = lax.broadcasted_iota(jnp.int32, (chunk, chunk), 0)
    tc = lax.broadcasted_iota(jnp.int32, (chunk, chunk), 1)
    strict = tr > tc
    incl = tr >= tc

    w_log = -_softplus(-(w0 + lw_ref[0])) - 0.5
    log_decay = -jnp.exp(w_log)
    if t_valid < chunk:
        trow = lax.broadcasted_iota(jnp.int32, (chunk, 1), 0)
        log_decay = jnp.where(trow < t_valid, log_decay, 0.0)
    a = jax.nn.sigmoid(a0 + la_ref[0])
    kk = k * k_k
    k2 = k * (1.0 + (a - 1.0) * k_a)
    sums = _dot_ones_rhs(jnp.concatenate([kk * kk, r * k2 * r_k], axis=0), same_head_ones)
    kk = kk / jnp.maximum(jnp.sqrt(sums[:chunk]), 1e-12)
    bonus = sums[chunk:] * v
    b = kk * a

    cs = _dot_ones_lhs(incl.astype(BF16), log_decay)
    cs_last = cs[chunk - 1:chunk, :]
    at = -kk * jnp.exp(cs - log_decay)
    ginv = jnp.exp(-cs)
    bt = b * ginv
    kt = k2 * ginv
    rt = r * jnp.exp(cs)
    to_end = jnp.exp(cs_last - cs)
    t0 = state_ref[...]

    eye_c = (tr == tc).astype(F32)
    decay_diag = jnp.where(ri == ci, jnp.broadcast_to(jnp.exp(cs_last), (LANES, LANES)), 0.0)
    from_state = _dot3(jnp.concatenate([at, rt, decay_diag], axis=0), t0)
    rhs = from_state[:chunk]
    y = from_state[chunk:2 * chunk]
    keys = jnp.concatenate([bt, kt], axis=0)
    a_ab, a_rb = [], []
    for m in head_masks:
        quad = _dot3(jnp.concatenate([at * m, rt * m], axis=0), keys, _NT)
        vh = v * m
        a_ab.append(jnp.where(strict, quad[:chunk, :chunk], 0.0))
        a_rb.append(jnp.where(incl, quad[chunk:, :chunk], 0.0))
        rhs = rhs + _dot3(jnp.where(strict, quad[:chunk, chunk:], 0.0), vh)
        y = y + _dot3(jnp.where(incl, quad[chunk:, chunk:], 0.0), vh)

    u = jnp.zeros_like(rhs)
    for m, low in zip(head_masks, a_ab):
        inv = eye_c + low
        power = low
        span = 2
        while span < chunk:
            power = _dot3(power, power)
            inv = inv + _dot3(inv, power)
            span *= 2
        u = u + _dot3(inv, rhs * m)
    for m, arb in zip(head_masks, a_rb):
        y = y + _dot3(arb, u * m)

    outer = _dot3(jnp.concatenate([b * to_end, k2 * to_end], axis=0),
                  jnp.concatenate([u, v], axis=0), _TN)
    state_ref[...] = from_state[2 * chunk:] + jnp.where(same_head, outer, 0.0)

    inv_n = 1.0 / RW_HEAD_DIM
    mu = _dot_ones_rhs(y, same_head_ones) * inv_n
    d = y - mu
    var = _dot_ones_rhs(d * d, same_head_ones) * inv_n
    yn = d * lax.rsqrt(var + GN_EPS) * lnx_w + lnx_b
    y_ref[0] = (yn + bonus) * g_ref[0]

    @pl.when(c == n_chunks - 1)
    def _():
        st_ref[0, 0] = state_ref[...]


def rwkv7_scan(r, k, v, lora_w, lora_a, g, params, state_t, chunk, t_valid):
    batch, t, _ = r.shape
    n_chunks = t // chunk
    seq_spec = pl.BlockSpec((1, chunk, LANES), lambda b, p, c: (b, c, p))
    st_spec = pl.BlockSpec((1, 1, LANES, LANES), lambda b, p, c: (b, p, 0, 0))
    kernel = functools.partial(_rwkv_kernel, chunk=chunk, t_valid=t_valid, n_chunks=n_chunks)
    return pl.pallas_call(
        kernel,
        grid=(batch, RW_PAIRS, n_chunks),
        in_specs=[seq_spec] * 6 + [pl.BlockSpec((8, LANES), lambda b, p, c: (0, p)), st_spec],
        out_specs=[seq_spec, st_spec],
        out_shape=[jax.ShapeDtypeStruct((batch, t, RW_WIDTH), F32),
                   jax.ShapeDtypeStruct((batch, RW_PAIRS, LANES, LANES), F32)],
        scratch_shapes=[pltpu.VMEM((LANES, LANES), F32)],
        compiler_params=pltpu.CompilerParams(
            dimension_semantics=("parallel", "parallel", "arbitrary"),
            vmem_limit_bytes=VMEM_LIMIT),
        name="rwkv7_scan",
    )(r, k, v, lora_w, lora_a, g, params, state_t)


def _pack_state(s):
    batch = s.shape[0]
    st = jnp.swapaxes(s, -1, -2).reshape(batch, RW_PAIRS, 2, RW_HEAD_DIM, RW_HEAD_DIM)
    z = jnp.zeros_like(st[:, :, 0])
    top = jnp.concatenate([st[:, :, 0], z], axis=-1)
    bot = jnp.concatenate([z, st[:, :, 1]], axis=-1)
    return jnp.concatenate([top, bot], axis=-2)


def _unpack_state(t):
    batch = t.shape[0]
    d = RW_HEAD_DIM
    pair = jnp.stack([t[:, :, :d, :d], t[:, :, d:, d:]], axis=2)
    return jnp.swapaxes(pair.reshape(batch, RW_HEADS, d, d), -1, -2)


def rwkv7_branch(cols, shift_prev, wkv_prev, mu_rw, params, w2, a2, g2, chunk, t_pad):
    batch, t, _ = cols.shape
    prev = jnp.concatenate([shift_prev[:, None], cols[:, :-1]], axis=1)
    mixed = cols + (prev - cols) * mu_rw
    w = RW_WIDTH
    r, k, v = mixed[..., :w], mixed[..., w:2 * w], mixed[..., 2 * w:3 * w]
    wd = mixed[..., 3 * w:3 * w + DECAY_LORA]
    ad = mixed[..., 3 * w + DECAY_LORA:3 * w + DECAY_LORA + ICLR_LORA]
    gd = mixed[..., 3 * w + DECAY_LORA + ICLR_LORA:]
    m = batch * t
    tm = min(512, m)
    lora_w = matmul(jnp.tanh(wd).reshape(m, -1).astype(BF16), w2, F32, tm=tm, tn=512, name="lora_w")
    lora_a = matmul(ad.reshape(m, -1).astype(BF16), a2, F32, tm=tm, tn=512, name="lora_a")
    g = matmul(jax.nn.sigmoid(gd).reshape(m, -1).astype(BF16), g2, F32, tm=tm, tn=512, name="lora_g")

    def seq(z):
        z = z.reshape(batch, t, w)
        if t_pad > t:
            z = jnp.pad(z, ((0, 0), (0, t_pad - t), (0, 0)))
        return z

    y, st = rwkv7_scan(seq(r), seq(k), seq(v), seq(lora_w), seq(lora_a), seq(g), params,
                       _pack_state(wkv_prev), chunk, min(t, chunk))
    return y[:, :t].reshape(m, w), _unpack_state(st)


def kernel(x_prompt, x_sample, cache_k, cache_v, state_wkv, state_shift, page_table, ln1_g, w_in, sb_bias, mu_rw, w0, w2, a0, a2, g2, k_k, k_a, r_k, lnx_w, lnx_b, w_br_sb, w_br_rw, w_out, ln2_g, w_up, w_down, lnf_g):
    bp, seq, d_model = x_prompt.shape
    bs, dec, _ = x_sample.shape
    mp, ms = bp * seq, bs * dec
    tm = 512

    x = jnp.concatenate([x_prompt.reshape(mp, d_model), x_sample.reshape(ms, d_model)], axis=0)
    h = rms_norm(x, ln1_g[0], BF16)

    w_in0 = w_in[0]
    qkv_w = w_in0[:, :3 * SB_WIDTH].astype(BF16)
    rw_cols_pad = 3584
    rw_w = jnp.pad(w_in0[:, 3 * SB_WIDTH:3 * SB_WIDTH + RW_PROJ],
                   ((0, 0), (0, rw_cols_pad - RW_PROJ))).astype(BF16)
    gate_w = w_in0[:, 3 * SB_WIDTH + RW_PROJ:].astype(BF16)

    qkv = matmul(h, qkv_w, F32, tm=tm, tn=512, name="proj_qkv")
    rw_cols = matmul(h, rw_w, F32, tm=tm, tn=512, name="proj_rw")[:, :RW_PROJ]
    gates = matmul(h, gate_w, F32, tm=tm, tn=512, name="proj_gates")

    y_sb_p = sb_prompt_attention(qkv, sb_bias[0], bp, seq)
    qkv_s = qkv[mp:].reshape(bs, dec, 3 * SB_WIDTH)
    q_s, k_s, v_s = (qkv_s[..., i * SB_WIDTH:(i + 1) * SB_WIDTH] for i in range(3))
    y_sb_s = sb_sample_attention(
        q_s, k_s, v_s, cache_k, cache_v, page_table, sb_bias[0])
    y_sb = jnp.concatenate([y_sb_p, y_sb_s.reshape(ms, SB_WIDTH)], axis=0).astype(BF16)

    params = jnp.stack([w0[0], a0[0], k_k[0], k_a[0], r_k[0].reshape(-1), lnx_w[0], lnx_b[0],
                        jnp.zeros((RW_WIDTH,), F32)])
    w2b, a2b, g2b = w2[0].astype(BF16), a2[0].astype(BF16), g2[0].astype(BF16)
    cols_p = rw_cols[:mp].reshape(bp, seq, RW_PROJ)
    cols_s = rw_cols[mp:].reshape(bs, dec, RW_PROJ)
    y_rw_p, wkv_p = rwkv7_branch(
        cols_p, jnp.zeros((bp, RW_PROJ), F32),
        jnp.zeros((bp, RW_HEADS, RW_HEAD_DIM, RW_HEAD_DIM), F32),
        mu_rw[0], params, w2b, a2b, g2b, chunk=128, t_pad=seq)
    y_rw_s, wkv_s = rwkv7_branch(
        cols_s, state_shift[0], state_wkv[0], mu_rw[0], params, w2b, a2b, g2b, chunk=8, t_pad=8)
    y_rw = jnp.concatenate([y_rw_p, y_rw_s], axis=0).astype(BF16)

    gate_sb, gate_rw = gates[:, :d_model], gates[:, d_model:]
    m_sb = matmul(y_sb, w_br_sb[0].astype(BF16), F32, tm=tm, tn=512, extras=(gate_sb,),
                  epilogue=lambda acc, gt: jax.nn.sigmoid(gt) * acc, name="branch_sb")
    merged = matmul(y_rw, w_br_rw[0].astype(BF16), BF16, tm=tm, tn=512, extras=(gate_rw, m_sb),
                    epilogue=lambda acc, gt, prev: prev + jax.nn.sigmoid(gt) * acc,
                    name="branch_rw")
    x1 = matmul(merged, w_out[0].astype(BF16), F32, tm=tm, tn=512, extras=(x,),
                epilogue=lambda acc, res: res + acc, name="out_proj")
    h2 = rms_norm(x1, ln2_g[0], BF16)
    up = matmul(h2, w_up[0].astype(BF16), BF16, tm=tm, tn=1024,
                epilogue=lambda acc: jnp.square(jnp.maximum(acc, 0.0)), name="mlp_up")
    x2 = matmul(up, w_down[0].astype(BF16), F32, tm=tm, tn=512, tk=2048, extras=(x1,),
                epilogue=lambda acc, res: res + acc, name="mlp_down")
    y = rms_norm(x2, lnf_g, F32)

    def kv_out(rows, lo, b, t):
        return rows[:, lo:lo + SB_WIDTH].reshape(1, b, t, SB_HEADS, SB_HEAD_DIM)

    return (y[:mp].reshape(bp, seq, d_model),
            y[mp:].reshape(bs, dec, d_model),
            kv_out(qkv[:mp], SB_WIDTH, bp, seq),
            kv_out(qkv[:mp], 2 * SB_WIDTH, bp, seq),
            wkv_p[None],
            cols_p[:, -1][None],
            kv_out(qkv[mp:], SB_WIDTH, bs, dec),
            kv_out(qkv[mp:], 2 * SB_WIDTH, bs, dec),
            wkv_s[None],
            cols_s[:, -1][None])
```

```python
import functools

import jax
import jax.numpy as jnp
from jax import lax
from jax.experimental import pallas as pl
from jax.experimental.pallas import tpu as pltpu

F32 = jnp.float32
BF16 = jnp.bfloat16
HIGHEST = lax.Precision.HIGHEST

LANES = 128
SB_HEADS = 8
SB_HEAD_DIM = 128
SB_WIDTH = SB_HEADS * SB_HEAD_DIM
RW_HEADS = 16
RW_HEAD_DIM = 64
RW_WIDTH = RW_HEADS * RW_HEAD_DIM
RW_PAIRS = RW_WIDTH // LANES
DECAY_LORA = 64
ICLR_LORA = 64
GATE_LORA = 160
RW_PROJ = 3 * RW_WIDTH + DECAY_LORA + ICLR_LORA + GATE_LORA
RMS_EPS = 1e-6
GN_EPS = 64e-5
PAGE_SIZE = 128
VMEM_LIMIT = 48 * 1024 * 1024


def _nt(a, b, precision=None):
    return lax.dot_general(a, b, (((1,), (1,)), ((), ())), precision=precision,
                           preferred_element_type=F32)


def _tn(a, b, precision=None):
    return lax.dot_general(a, b, (((0,), (0,)), ((), ())), precision=precision,
                           preferred_element_type=F32)


def _softplus(z):
    return jnp.maximum(z, 0.0) + jnp.log1p(jnp.exp(-jnp.abs(z)))


def _rms_kernel(x_ref, g_ref, o_ref):
    x = x_ref[...]
    y = x * lax.rsqrt(jnp.mean(x * x, axis=-1, keepdims=True) + RMS_EPS)
    o_ref[...] = (y * g_ref[...]).astype(o_ref.dtype)


def rms_norm(x, g, out_dtype, tm=256):
    m, d = x.shape
    return pl.pallas_call(
        _rms_kernel,
        grid=(m // tm,),
        in_specs=[pl.BlockSpec((tm, d), lambda i: (i, 0)),
                  pl.BlockSpec((1, d), lambda i: (0, 0))],
        out_specs=pl.BlockSpec((tm, d), lambda i: (i, 0)),
        out_shape=jax.ShapeDtypeStruct((m, d), out_dtype),
        compiler_params=pltpu.CompilerParams(dimension_semantics=("parallel",)),
        name="rms_norm",
    )(x, g.reshape(1, d))


def _mm_kernel(*refs, nk, n_extra, epilogue):
    a_ref, b_ref = refs[0], refs[1]
    extras = refs[2:2 + n_extra]
    o_ref = refs[2 + n_extra]

    def finish(acc):
        if epilogue is not None:
            acc = epilogue(acc, *[e[...] for e in extras])
        o_ref[...] = acc.astype(o_ref.dtype)

    if nk == 1:
        finish(jnp.dot(a_ref[...], b_ref[...], preferred_element_type=F32))
        return
    acc_ref = refs[3 + n_extra]
    k = pl.program_id(2)

    @pl.when(k == 0)
    def _():
        acc_ref[...] = jnp.zeros_like(acc_ref)

    acc_ref[...] += jnp.dot(a_ref[...], b_ref[...], preferred_element_type=F32)

    @pl.when(k == nk - 1)
    def _():
        finish(acc_ref[...])


def matmul(a, b, out_dtype, *, tm, tn, tk=None, extras=(), epilogue=None, name="matmul"):
    m, kdim = a.shape
    _, n = b.shape
    tk = kdim if tk is None else tk
    nk = kdim // tk
    assert m % tm == 0 and n % tn == 0 and kdim % tk == 0
    in_specs = [pl.BlockSpec((tm, tk), lambda i, j, k: (i, k)),
                pl.BlockSpec((tk, tn), lambda i, j, k: (k, j))]
    in_specs += [pl.BlockSpec((tm, tn), lambda i, j, k: (i, j)) for _ in extras]
    scratch = [pltpu.VMEM((tm, tn), F32)] if nk > 1 else []
    return pl.pallas_call(
        functools.partial(_mm_kernel, nk=nk, n_extra=len(extras), epilogue=epilogue),
        grid=(m // tm, n // tn, nk),
        in_specs=in_specs,
        out_specs=pl.BlockSpec((tm, tn), lambda i, j, k: (i, j)),
        out_shape=jax.ShapeDtypeStruct((m, n), out_dtype),
        scratch_shapes=scratch,
        compiler_params=pltpu.CompilerParams(
            dimension_semantics=("parallel", "parallel", "arbitrary"),
            vmem_limit_bytes=VMEM_LIMIT),
        name=name,
    )(a, b, *extras)


def _split(x):
    hi = x.astype(BF16)
    lo = (x - hi.astype(F32)).astype(BF16)
    return hi, lo


def _sb_weights(z, causal, tri, carry, out_dtype=BF16):
    n = z.shape[0]
    l1m = -_softplus(z)
    if causal is not None:
        l1m = jnp.where(causal, l1m, 0.0)
    hi, lo = _split(l1m)
    both = jnp.dot(jnp.concatenate([hi, lo], axis=0), tri, preferred_element_type=F32)
    between = both[:n] + both[n:]
    w = jnp.exp(z + l1m + between + carry)
    if causal is not None:
        w = jnp.where(causal, w, 0.0)
    return w.astype(out_dtype), jnp.sum(l1m, axis=1, keepdims=True)


def _strict_upper_ones(n):
    row = lax.broadcasted_iota(jnp.int32, (n, n), 0)
    col = lax.broadcasted_iota(jnp.int32, (n, n), 1)
    return (row > col).astype(BF16)


def _sb_prompt_kernel(bias_ref, q_ref, k_ref, v_ref, o_ref, *carry_refs, tq, tk, heads, scale):
    grp = pl.program_id(1)
    qi = pl.program_id(2)
    d = SB_HEAD_DIM
    row = lax.broadcasted_iota(jnp.int32, (tq, tk), 0)
    col = lax.broadcasted_iota(jnp.int32, (tq, tk), 1)
    tri = _strict_upper_ones(tk)
    (carry_ref,) = carry_refs

    def by_head(x):
        return jnp.stack([x[:, h * d:(h + 1) * d] for h in range(heads)])

    q = by_head(q_ref[...]).astype(BF16)
    bias = jnp.concatenate(
        [jnp.full((tq, 1), bias_ref[grp * heads + h], F32) for h in range(heads)], axis=0)
    o_ref[...] = jnp.zeros_like(o_ref)
    carry_ref[...] = jnp.zeros_like(carry_ref)
    n_blocks = (qi + 1) * (tq // tk)

    def body(jj, _):
        j = n_blocks - 1 - jj
        start = pl.multiple_of(j * tk, tk)
        causal = (col + j * tk) < (row + qi * tq)
        causal = jnp.concatenate([causal] * heads, axis=0)
        kb = by_head(k_ref[pl.ds(start, tk), :]).astype(BF16)
        vb = by_head(v_ref[pl.ds(start, tk), :]).astype(BF16)
        z = lax.dot_general(q, kb, _BNT, preferred_element_type=F32)
        z = z.reshape(heads * tq, tk) * scale + bias
        w, rowsum = _sb_weights(z, causal, tri, carry_ref[...], F32)
        out = lax.dot_general(w.reshape(heads, tq, tk).astype(BF16), vb, _BNN,
                              preferred_element_type=F32)
        for h in range(heads):
            o_ref[:, h * d:(h + 1) * d] += out[h]
        carry_ref[...] += rowsum
        return 0

    lax.fori_loop(0, n_blocks, body, 0)


def sb_prompt_attention(qkv, sb_bias, batch, seq, tq=256, tk=128, heads=4):
    nq = seq // tq
    groups = SB_HEADS // heads
    width = heads * SB_HEAD_DIM
    kernel = functools.partial(_sb_prompt_kernel, tq=tq, tk=tk, heads=heads,
                               scale=SB_HEAD_DIM ** -0.5)
    return pl.pallas_call(
        kernel,
        grid=(batch, groups, nq),
        in_specs=[
            pl.BlockSpec(memory_space=pltpu.SMEM),
            pl.BlockSpec((tq, width), lambda b, g, i: (b * nq + i, g)),
            pl.BlockSpec((seq, width), lambda b, g, i: (b, groups + g)),
            pl.BlockSpec((seq, width), lambda b, g, i: (b, 2 * groups + g)),
        ],
        out_specs=pl.BlockSpec((tq, width), lambda b, g, i: (b * nq + i, g)),
        out_shape=jax.ShapeDtypeStruct((batch * seq, SB_WIDTH), F32),
        scratch_shapes=[pltpu.VMEM((heads * tq, 1), F32)],
        compiler_params=pltpu.CompilerParams(
            dimension_semantics=("parallel", "parallel", "arbitrary"),
            vmem_limit_bytes=VMEM_LIMIT),
        name="sb_prompt_attention",
    )(sb_bias, qkv, qkv, qkv)


SAMPLE_ROWS = 8


def _sb_sample_kernel(pt_ref, q_ref, bias_ref, *refs, n_steps, n_new, pages_per_step, scale):
    del pt_ref
    page_refs = refs[:2 * pages_per_step]
    kn_ref, vn_ref, o_ref, acc_ref, carry_ref, kpad_ref, vpad_ref = refs[2 * pages_per_step:]
    s = pl.program_id(1)
    d = SB_HEAD_DIM
    rr = SAMPLE_ROWS
    tri = _strict_upper_ones(PAGE_SIZE)
    q = q_ref[0]

    qb = q.reshape(SB_HEADS, rr, d)
    rows = SB_HEADS * rr

    def scores(k_heads, n_blocks):
        qq = qb if n_blocks == 1 else jnp.concatenate([qb] * n_blocks, axis=0)
        z = lax.dot_general(qq, k_heads, _BNT, preferred_element_type=F32)
        return z.reshape(n_blocks, rows, PAGE_SIZE) * scale + bias_ref[...]

    def values(w, v_heads, n_blocks):
        wb = w.reshape(n_blocks * SB_HEADS, rr, PAGE_SIZE).astype(BF16)
        out = lax.dot_general(wb, v_heads, _BNN, preferred_element_type=F32)
        return out.reshape(n_blocks, rows, d)

    def page_heads(refs):
        return jnp.stack([ref[pl.ds(h, PAGE_SIZE, stride=SB_HEADS), :].astype(BF16)
                          for ref in refs for h in range(SB_HEADS)])

    def pad_heads(ref):
        return jnp.stack([ref[:, h * d:(h + 1) * d].astype(BF16) for h in range(SB_HEADS)])

    @pl.when(s == 0)
    def _():
        kpad_ref[...] = jnp.zeros_like(kpad_ref)
        vpad_ref[...] = jnp.zeros_like(vpad_ref)
        kpad_ref[0:n_new, :] = kn_ref[0]
        vpad_ref[0:n_new, :] = vn_ref[0]
        row = lax.broadcasted_iota(jnp.int32, (rows, PAGE_SIZE), 0)
        col = lax.broadcasted_iota(jnp.int32, (rows, PAGE_SIZE), 1)
        causal = col < row % rr
        w, rowsum = _sb_weights(scores(pad_heads(kpad_ref), 1)[0], causal, tri, 0.0, BF16)
        acc_ref[...] = values(w, pad_heads(vpad_ref), 1)[0]
        carry_ref[...] = rowsum

    z = scores(page_heads(page_refs[0::2]), pages_per_step)
    carry = carry_ref[...]
    ws = []
    for i in range(pages_per_step):
        w, rowsum = _sb_weights(z[i], None, tri, carry, F32)
        ws.append(w)
        carry = carry + rowsum
    out = values(jnp.stack(ws), page_heads(page_refs[1::2]), pages_per_step)
    acc_ref[...] += jnp.sum(out, axis=0)
    carry_ref[...] = carry

    @pl.when(s == n_steps - 1)
    def _():
        for h in range(SB_HEADS):
            o_ref[0, :, h * d:(h + 1) * d] = acc_ref[h * rr:h * rr + n_new, :]


def sb_sample_attention(q, k_new, v_new, cache_k, cache_v, page_table, sb_bias):
    seqs, n_new, _ = q.shape
    n_pages = page_table.shape[1]
    pps = 4
    assert n_pages % pps == 0 and n_new <= SAMPLE_ROWS
    n_steps = n_pages // pps
    rows = SB_HEADS * SAMPLE_ROWS
    qh = jnp.swapaxes(q.reshape(seqs, n_new, SB_HEADS, SB_HEAD_DIM), 1, 2)
    qh = jnp.pad(qh, ((0, 0), (0, 0), (0, SAMPLE_ROWS - n_new), (0, 0)))
    qh = qh.reshape(seqs, rows, SB_HEAD_DIM).astype(BF16)
    bias = jnp.broadcast_to(jnp.repeat(sb_bias, SAMPLE_ROWS)[:, None], (rows, PAGE_SIZE)).astype(F32)

    def page_spec(offset):
        return pl.BlockSpec(
            (None, None, PAGE_SIZE * SB_HEADS, SB_HEAD_DIM),
            lambda b, s, pt: (0, pt[b, n_pages - 1 - offset - pps * s], 0, 0))

    new_spec = pl.BlockSpec((1, n_new, SB_WIDTH), lambda b, s, pt: (b, 0, 0))
    depth, n_pool = cache_k.shape[:2]
    cache_k = cache_k.reshape(depth, n_pool, PAGE_SIZE * SB_HEADS, SB_HEAD_DIM)
    cache_v = cache_v.reshape(depth, n_pool, PAGE_SIZE * SB_HEADS, SB_HEAD_DIM)
    kernel = functools.partial(_sb_sample_kernel, n_steps=n_steps, n_new=n_new,
                               pages_per_step=pps, scale=SB_HEAD_DIM ** -0.5)
    page_specs = [page_spec(i // 2) for i in range(2 * pps)]
    return pl.pallas_call(
        kernel,
        grid_spec=pltpu.PrefetchScalarGridSpec(
            num_scalar_prefetch=1,
            grid=(seqs, n_steps),
            in_specs=[
                pl.BlockSpec((1, rows, SB_HEAD_DIM), lambda b, s, pt: (b, 0, 0)),
                pl.BlockSpec((rows, PAGE_SIZE), lambda b, s, pt: (0, 0)),
                *page_specs, new_spec, new_spec,
            ],
            out_specs=new_spec,
            scratch_shapes=[
                pltpu.VMEM((rows, SB_HEAD_DIM), F32),
                pltpu.VMEM((rows, 1), F32),
                pltpu.VMEM((PAGE_SIZE, SB_WIDTH), F32),
                pltpu.VMEM((PAGE_SIZE, SB_WIDTH), F32),
            ],
        ),
        out_shape=jax.ShapeDtypeStruct((seqs, n_new, SB_WIDTH), F32),
        compiler_params=pltpu.CompilerParams(
            dimension_semantics=("parallel", "arbitrary"),
            vmem_limit_bytes=VMEM_LIMIT),
        name="sb_sample_attention",
    )(page_table, qh, bias, *([cache_k, cache_v] * pps), k_new, v_new)


_NN = (((1,), (0,)), ((), ()))
_NT = (((1,), (1,)), ((), ()))
_TN = (((0,), (0,)), ((), ()))
_BNN = (((2,), (1,)), ((0,), (0,)))
_BNT = (((2,), (2,)), ((0,), (0,)))
_BTN = (((1,), (1,)), ((0,), (0,)))


def _dot3(a, b, dims=_NN):
    ah, al = _split(a)
    bh, bl = _split(b)

    def d(x, y):
        return lax.dot_general(x, y, dims, preferred_element_type=F32)

    return d(ah, bh) + (d(al, bh) + d(ah, bl))


def _dot_ones_rhs(a, ones):
    n = a.shape[0]
    ah, al = _split(a)
    both = jnp.dot(jnp.concatenate([ah, al], axis=0), ones, preferred_element_type=F32)
    return both[:n] + both[n:]


def _dot_ones_lhs(ones, b):
    n = b.shape[1]
    bh, bl = _split(b)
    both = jnp.dot(ones, jnp.concatenate([bh, bl], axis=1), preferred_element_type=F32)
    return both[:, :n] + both[:, n:]


def _rwkv_chunk(r, k, v, lora_w, lora_a, g, prm, t0, *, chunk, t_valid):
    nb = r.shape[0]
    w0, a0, k_k, k_a, r_k, lnx_w, lnx_b = (prm[:, i:i + 1, :] for i in range(7))

    lane = lax.broadcasted_iota(jnp.int32, (1, 1, LANES), 2)
    head_masks = ((lane < RW_HEAD_DIM).astype(F32), (lane >= RW_HEAD_DIM).astype(F32))
    ri = lax.broadcasted_iota(jnp.int32, (LANES, LANES), 0)
    ci = lax.broadcasted_iota(jnp.int32, (LANES, LANES), 1)
    same_head = (ri // RW_HEAD_DIM) == (ci // RW_HEAD_DIM)
    same_head_ones = same_head.astype(BF16)
    tr = lax.broadcasted_iota(jnp.int32, (chunk, chunk), 0)
    tc = lax.broadcasted_iota(jnp.int32, (chunk, chunk), 1)
    strict = tr > tc
    incl = tr >= tc

    def head_sums(x):
        rows = x.shape[0] * x.shape[1]
        return _dot_ones_rhs(x.reshape(rows, LANES), same_head_ones).reshape(x.shape)

    def both_heads(x):
        return jnp.concatenate([x * m for m in head_masks], axis=0)

    def sum_heads(x):
        return x[:nb] + x[nb:]

    w_log = -_softplus(-(w0 + lora_w)) - 0.5
    log_decay = -jnp.exp(w_log)
    if t_valid < chunk:
        trow = lax.broadcasted_iota(jnp.int32, (1, chunk, 1), 1)
        log_decay = jnp.where(trow < t_valid, log_decay, 0.0)
    a = jax.nn.sigmoid(a0 + lora_a)
    kk = k * k_k
    k2 = k * (1.0 + (a - 1.0) * k_a)
    sums = head_sums(jnp.concatenate([kk * kk, r * k2 * r_k], axis=0))
    kk = kk / jnp.maximum(jnp.sqrt(sums[:nb]), 1e-12)
    bonus = sums[nb:] * v
    b = kk * a

    ld_hi, ld_lo = _split(log_decay)
    incl_b = jnp.broadcast_to(incl.astype(BF16)[None], (nb, chunk, chunk))
    cs2 = lax.dot_general(incl_b, jnp.concatenate([ld_hi, ld_lo], axis=2), _BNN,
                          preferred_element_type=F32)
    cs = cs2[:, :, :LANES] + cs2[:, :, LANES:]
    cs_last = cs[:, chunk - 1:chunk, :]
    at = -kk * jnp.exp(cs - log_decay)
    ginv = jnp.exp(-cs)
    bt = b * ginv
    kt = k2 * ginv
    rt = r * jnp.exp(cs)
    to_end = jnp.exp(cs_last - cs)

    eye_c = (tr == tc).astype(F32)
    decay_diag = jnp.where((ri == ci)[None], jnp.broadcast_to(jnp.exp(cs_last), (nb, LANES, LANES)), 0.0)
    from_state = _dot3(jnp.concatenate([at, rt, decay_diag], axis=1), t0, _BNN)
    rhs = from_state[:, :chunk]
    y = from_state[:, chunk:2 * chunk]

    keys = jnp.concatenate([bt, kt], axis=1)
    quad = _dot3(both_heads(jnp.concatenate([at, rt], axis=1)),
                 jnp.concatenate([keys, keys], axis=0), _BNT)
    vh = both_heads(v)
    a_ab = jnp.where(strict[None], quad[:, :chunk, :chunk], 0.0)
    a_rb = jnp.where(incl[None], quad[:, chunk:, :chunk], 0.0)
    a_ak = jnp.where(strict[None], quad[:, :chunk, chunk:], 0.0)
    a_rk = jnp.where(incl[None], quad[:, chunk:, chunk:], 0.0)
    rhs = rhs + sum_heads(_dot3(a_ak, vh, _BNN))
    y = y + sum_heads(_dot3(a_rk, vh, _BNN))

    power = a_ab
    inv = eye_c[None] + power
    span = 2
    while span < chunk:
        power = _dot3(power, power, _BNN)
        inv = inv + _dot3(inv, power, _BNN)
        span *= 2
    u = sum_heads(_dot3(inv, both_heads(rhs), _BNN))
    y = y + sum_heads(_dot3(a_rb, both_heads(u), _BNN))

    outer = _dot3(jnp.concatenate([b * to_end, k2 * to_end], axis=1),
                  jnp.concatenate([u, v], axis=1), _BTN)
    new_state = from_state[:, 2 * chunk:] + jnp.where(same_head[None], outer, 0.0)

    inv_n = 1.0 / RW_HEAD_DIM
    mu = head_sums(y) * inv_n
    d = y - mu
    var = head_sums(d * d) * inv_n
    yn = d * lax.rsqrt(var + GN_EPS) * lnx_w + lnx_b
    return (yn + bonus) * g, new_state


def _rwkv_kernel(r_ref, k_ref, v_ref, lw_ref, la_ref, g_ref, prm_ref, s0_ref, y_ref, st_ref,
                 state_ref, *, chunk, t_valid, n_chunks, pairs):
    c = pl.program_id(2)

    @pl.when(c == 0)
    def _():
        state_ref[...] = s0_ref[0]

    def by_pair(ref2d):
        return jnp.stack([ref2d[:, p * LANES:(p + 1) * LANES] for p in range(pairs)])

    out, new_state = _rwkv_chunk(
        by_pair(r_ref[0]), by_pair(k_ref[0]), by_pair(v_ref[0]), by_pair(lw_ref[0]),
        by_pair(la_ref[0]), by_pair(g_ref[0]), by_pair(prm_ref[...]), state_ref[...],
        chunk=chunk, t_valid=t_valid)
    for p in range(pairs):
        y_ref[0, :, p * LANES:(p + 1) * LANES] = out[p]
    state_ref[...] = new_state

    @pl.when(c == n_chunks - 1)
    def _():
        st_ref[0] = state_ref[...]


def rwkv7_scan(r, k, v, lora_w, lora_a, g, params, state_t, chunk, t_valid, pairs):
    batch, t, _ = r.shape
    n_chunks = t // chunk
    width = pairs * LANES
    seq_spec = pl.BlockSpec((1, chunk, width), lambda b, p, c: (b, c, p))
    st_spec = pl.BlockSpec((1, pairs, LANES, LANES), lambda b, p, c: (b, p, 0, 0))
    kernel = functools.partial(_rwkv_kernel, chunk=chunk, t_valid=t_valid, n_chunks=n_chunks,
                               pairs=pairs)
    return pl.pallas_call(
        kernel,
        grid=(batch, RW_PAIRS // pairs, n_chunks),
        in_specs=[seq_spec] * 6 + [pl.BlockSpec((8, width), lambda b, p, c: (0, p)), st_spec],
        out_specs=[seq_spec, st_spec],
        out_shape=[jax.ShapeDtypeStruct((batch, t, RW_WIDTH), F32),
                   jax.ShapeDtypeStruct((batch, RW_PAIRS, LANES, LANES), F32)],
        scratch_shapes=[pltpu.VMEM((pairs, LANES, LANES), F32)],
        compiler_params=pltpu.CompilerParams(
            dimension_semantics=("parallel", "parallel", "arbitrary"),
            vmem_limit_bytes=VMEM_LIMIT),
        name="rwkv7_scan",
    )(r, k, v, lora_w, lora_a, g, params, state_t)


def _pack_state(s):
    batch = s.shape[0]
    st = jnp.swapaxes(s, -1, -2).reshape(batch, RW_PAIRS, 2, RW_HEAD_DIM, RW_HEAD_DIM)
    z = jnp.zeros_like(st[:, :, 0])
    top = jnp.concatenate([st[:, :, 0], z], axis=-1)
    bot = jnp.concatenate([z, st[:, :, 1]], axis=-1)
    return jnp.concatenate([top, bot], axis=-2)


def _unpack_state(t):
    batch = t.shape[0]
    d = RW_HEAD_DIM
    pair = jnp.stack([t[:, :, :d, :d], t[:, :, d:, d:]], axis=2)
    return jnp.swapaxes(pair.reshape(batch, RW_HEADS, d, d), -1, -2)


def rwkv7_branch(cols, shift_prev, wkv_prev, mu_rw, params, w2, a2, g2, chunk, t_pad, pairs):
    batch, t, _ = cols.shape
    prev = jnp.concatenate([shift_prev[:, None], cols[:, :-1]], axis=1)
    mixed = cols + (prev - cols) * mu_rw
    w = RW_WIDTH
    r, k, v = mixed[..., :w], mixed[..., w:2 * w], mixed[..., 2 * w:3 * w]
    wd = mixed[..., 3 * w:3 * w + DECAY_LORA]
    ad = mixed[..., 3 * w + DECAY_LORA:3 * w + DECAY_LORA + ICLR_LORA]
    gd = mixed[..., 3 * w + DECAY_LORA + ICLR_LORA:]
    m = batch * t
    tm = min(512, m)
    lora_w = matmul(jnp.tanh(wd).reshape(m, -1).astype(BF16), w2, F32, tm=tm, tn=512, name="lora_w")
    lora_a = matmul(ad.reshape(m, -1).astype(BF16), a2, F32, tm=tm, tn=512, name="lora_a")
    g = matmul(jax.nn.sigmoid(gd).reshape(m, -1).astype(BF16), g2, F32, tm=tm, tn=512, name="lora_g")

    def seq(z):
        z = z.reshape(batch, t, w)
        if t_pad > t:
            z = jnp.pad(z, ((0, 0), (0, t_pad - t), (0, 0)))
        return z

    y, st = rwkv7_scan(seq(r), seq(k), seq(v), seq(lora_w), seq(lora_a), seq(g), params,
                       _pack_state(wkv_prev), chunk, min(t, chunk), pairs)
    return y[:, :t].reshape(m, w), _unpack_state(st)


def kernel(x_prompt, x_sample, cache_k, cache_v, state_wkv, state_shift, page_table, ln1_g, w_in, sb_bias, mu_rw, w0, w2, a0, a2, g2, k_k, k_a, r_k, lnx_w, lnx_b, w_br_sb, w_br_rw, w_out, ln2_g, w_up, w_down, lnf_g):
    bp, seq, d_model = x_prompt.shape
    bs, dec, _ = x_sample.shape
    mp, ms = bp * seq, bs * dec
    tm = 512

    x = jnp.concatenate([x_prompt.reshape(mp, d_model), x_sample.reshape(ms, d_model)], axis=0)
    h = rms_norm(x, ln1_g[0], BF16)

    w_in0 = w_in[0]
    qkv_w = w_in0[:, :3 * SB_WIDTH].astype(BF16)
    rw_cols_pad = 3584
    rw_w = jnp.pad(w_in0[:, 3 * SB_WIDTH:3 * SB_WIDTH + RW_PROJ],
                   ((0, 0), (0, rw_cols_pad - RW_PROJ))).astype(BF16)
    gate_w = w_in0[:, 3 * SB_WIDTH + RW_PROJ:].astype(BF16)

    qkv = matmul(h, qkv_w, F32, tm=tm, tn=512, name="proj_qkv")
    rw_cols = matmul(h, rw_w, F32, tm=tm, tn=512, name="proj_rw")[:, :RW_PROJ]
    gates = matmul(h, gate_w, F32, tm=tm, tn=512, name="proj_gates")

    y_sb_p = sb_prompt_attention(qkv, sb_bias[0], bp, seq)
    qkv_s = qkv[mp:].reshape(bs, dec, 3 * SB_WIDTH)
    q_s, k_s, v_s = (qkv_s[..., i * SB_WIDTH:(i + 1) * SB_WIDTH] for i in range(3))
    y_sb_s = sb_sample_attention(
        q_s, k_s, v_s, cache_k, cache_v, page_table, sb_bias[0])
    y_sb = jnp.concatenate([y_sb_p, y_sb_s.reshape(ms, SB_WIDTH)], axis=0).astype(BF16)

    params = jnp.stack([w0[0], a0[0], k_k[0], k_a[0], r_k[0].reshape(-1), lnx_w[0], lnx_b[0],
                        jnp.zeros((RW_WIDTH,), F32)])
    w2b, a2b, g2b = w2[0].astype(BF16), a2[0].astype(BF16), g2[0].astype(BF16)
    cols_p = rw_cols[:mp].reshape(bp, seq, RW_PROJ)
    cols_s = rw_cols[mp:].reshape(bs, dec, RW_PROJ)
    y_rw_p, wkv_p = rwkv7_branch(
        cols_p, jnp.zeros((bp, RW_PROJ), F32),
        jnp.zeros((bp, RW_HEADS, RW_HEAD_DIM, RW_HEAD_DIM), F32),
        mu_rw[0], params, w2b, a2b, g2b, chunk=128, t_pad=seq, pairs=4)
    y_rw_s, wkv_s = rwkv7_branch(
        cols_s, state_shift[0], state_wkv[0], mu_rw[0], params, w2b, a2b, g2b, chunk=8, t_pad=8,
        pairs=8)
    y_rw = jnp.concatenate([y_rw_p, y_rw_s], axis=0).astype(BF16)

    gate_sb, gate_rw = gates[:, :d_model], gates[:, d_model:]
    m_sb = matmul(y_sb, w_br_sb[0].astype(BF16), F32, tm=tm, tn=512, extras=(gate_sb,),
                  epilogue=lambda acc, gt: jax.nn.sigmoid(gt) * acc, name="branch_sb")
    merged = matmul(y_rw, w_br_rw[0].astype(BF16), BF16, tm=tm, tn=512, extras=(gate_rw, m_sb),
                    epilogue=lambda acc, gt, prev: prev + jax.nn.sigmoid(gt) * acc,
                    name="branch_rw")
    x1 = matmul(merged, w_out[0].astype(BF16), F32, tm=tm, tn=512, extras=(x,),
                epilogue=lambda acc, res: res + acc, name="out_proj")
    h2 = rms_norm(x1, ln2_g[0], BF16)
    up = matmul(h2, w_up[0].astype(BF16), BF16, tm=tm, tn=1024,
                epilogue=lambda acc: jnp.square(jnp.maximum(acc, 0.0)), name="mlp_up")
    x2 = matmul(up, w_down[0].astype(BF16), F32, tm=tm, tn=512, tk=2048, extras=(x1,),
                epilogue=lambda acc, res: res + acc, name="mlp_down")
    y = rms_norm(x2, lnf_g, F32)

    def kv_out(rows, lo, b, t):
        return rows[:, lo:lo + SB_WIDTH].reshape(1, b, t, SB_HEADS, SB_HEAD_DIM)

    return (y[:mp].reshape(bp, seq, d_model),
            y[mp:].reshape(bs, dec, d_model),
            kv_out(qkv[:mp], SB_WIDTH, bp, seq),
            kv_out(qkv[:mp], 2 * SB_WIDTH, bp, seq),
            wkv_p[None],
            cols_p[:, -1][None],
            kv_out(qkv[mp:], SB_WIDTH, bs, dec),
            kv_out(qkv[mp:], 2 * SB_WIDTH, bs, dec),
            wkv_s[None],
            cols_s[:, -1][None])
```

```python
import functools

import jax
import jax.numpy as jnp
from jax import lax
from jax.experimental import pallas as pl
from jax.experimental.pallas import tpu as pltpu

F32 = jnp.float32
BF16 = jnp.bfloat16
HIGHEST = lax.Precision.HIGHEST

LANES = 128
SB_HEADS = 8
SB_HEAD_DIM = 128
SB_WIDTH = SB_HEADS * SB_HEAD_DIM
RW_HEADS = 16
RW_HEAD_DIM = 64
RW_WIDTH = RW_HEADS * RW_HEAD_DIM
RW_PAIRS = RW_WIDTH // LANES
DECAY_LORA = 64
ICLR_LORA = 64
GATE_LORA = 160
RW_PROJ = 3 * RW_WIDTH + DECAY_LORA + ICLR_LORA + GATE_LORA
RMS_EPS = 1e-6
GN_EPS = 64e-5
PAGE_SIZE = 128
VMEM_LIMIT = 48 * 1024 * 1024


def _nt(a, b, precision=None):
    return lax.dot_general(a, b, (((1,), (1,)), ((), ())), precision=precision,
                           preferred_element_type=F32)


def _tn(a, b, precision=None):
    return lax.dot_general(a, b, (((0,), (0,)), ((), ())), precision=precision,
                           preferred_element_type=F32)


def _softplus(z):
    return jnp.maximum(z, 0.0) + jnp.log1p(jnp.exp(-jnp.abs(z)))


def _rms_kernel(x_ref, g_ref, o_ref):
    x = x_ref[...]
    y = x * lax.rsqrt(jnp.mean(x * x, axis=-1, keepdims=True) + RMS_EPS)
    o_ref[...] = (y * g_ref[...]).astype(o_ref.dtype)


def rms_norm(x, g, out_dtype, tm=256):
    m, d = x.shape
    return pl.pallas_call(
        _rms_kernel,
        grid=(m // tm,),
        in_specs=[pl.BlockSpec((tm, d), lambda i: (i, 0)),
                  pl.BlockSpec((1, d), lambda i: (0, 0))],
        out_specs=pl.BlockSpec((tm, d), lambda i: (i, 0)),
        out_shape=jax.ShapeDtypeStruct((m, d), out_dtype),
        compiler_params=pltpu.CompilerParams(dimension_semantics=("parallel",)),
        name="rms_norm",
    )(x, g.reshape(1, d))


def _mm_kernel(*refs, nk, n_extra, epilogue):
    a_ref, b_ref = refs[0], refs[1]
    extras = refs[2:2 + n_extra]
    o_ref = refs[2 + n_extra]

    def finish(acc):
        if epilogue is not None:
            acc = epilogue(acc, *[e[...] for e in extras])
        o_ref[...] = acc.astype(o_ref.dtype)

    if nk == 1:
        finish(jnp.dot(a_ref[...], b_ref[...], preferred_element_type=F32))
        return
    acc_ref = refs[3 + n_extra]
    k = pl.program_id(2)

    @pl.when(k == 0)
    def _():
        acc_ref[...] = jnp.zeros_like(acc_ref)

    acc_ref[...] += jnp.dot(a_ref[...], b_ref[...], preferred_element_type=F32)

    @pl.when(k == nk - 1)
    def _():
        finish(acc_ref[...])


def matmul(a, b, out_dtype, *, tm, tn, tk=None, extras=(), extra_col_blocks=None, epilogue=None,
           name="matmul"):
    m, kdim = a.shape
    _, n = b.shape
    tk = kdim if tk is None else tk
    nk = kdim // tk
    assert m % tm == 0 and n % tn == 0 and kdim % tk == 0
    in_specs = [pl.BlockSpec((tm, tk), lambda i, j, k: (i, k)),
                pl.BlockSpec((tk, tn), lambda i, j, k: (k, j))]
    offsets = extra_col_blocks or (0,) * len(extras)
    in_specs += [pl.BlockSpec((tm, tn), lambda i, j, k, off=off: (i, j + off)) for off in offsets]
    scratch = [pltpu.VMEM((tm, tn), F32)] if nk > 1 else []
    return pl.pallas_call(
        functools.partial(_mm_kernel, nk=nk, n_extra=len(extras), epilogue=epilogue),
        grid=(m // tm, n // tn, nk),
        in_specs=in_specs,
        out_specs=pl.BlockSpec((tm, tn), lambda i, j, k: (i, j)),
        out_shape=jax.ShapeDtypeStruct((m, n), out_dtype),
        scratch_shapes=scratch,
        compiler_params=pltpu.CompilerParams(
            dimension_semantics=("parallel", "parallel", "arbitrary"),
            vmem_limit_bytes=VMEM_LIMIT),
        name=name,
    )(a, b, *extras)


def _split(x):
    hi = x.astype(BF16)
    lo = (x - hi.astype(F32)).astype(BF16)
    return hi, lo


def _sb_weights(z, causal, tri, carry, out_dtype=BF16):
    n = z.shape[0]
    l1m = -_softplus(z)
    if causal is not None:
        l1m = jnp.where(causal, l1m, 0.0)
    hi, lo = _split(l1m)
    both = jnp.dot(jnp.concatenate([hi, lo], axis=0), tri, preferred_element_type=F32)
    between = both[:n] + both[n:]
    w = jnp.exp(z + l1m + between + carry)
    if causal is not None:
        w = jnp.where(causal, w, 0.0)
    return w.astype(out_dtype), jnp.sum(l1m, axis=1, keepdims=True)


def _strict_upper_ones(n):
    row = lax.broadcasted_iota(jnp.int32, (n, n), 0)
    col = lax.broadcasted_iota(jnp.int32, (n, n), 1)
    return (row > col).astype(BF16)


def _sb_prompt_kernel(bias_ref, q_ref, k_ref, v_ref, o_ref, *carry_refs, tq, tk, heads, scale):
    grp = pl.program_id(1)
    qi = pl.program_id(2)
    d = SB_HEAD_DIM
    row = lax.broadcasted_iota(jnp.int32, (tq, tk), 0)
    col = lax.broadcasted_iota(jnp.int32, (tq, tk), 1)
    tri = _strict_upper_ones(tk)
    (carry_ref,) = carry_refs

    def by_head(x):
        return jnp.stack([x[:, h * d:(h + 1) * d] for h in range(heads)])

    q = by_head(q_ref[...]).astype(BF16)
    bias = jnp.concatenate(
        [jnp.full((tq, 1), bias_ref[grp * heads + h], F32) for h in range(heads)], axis=0)
    o_ref[...] = jnp.zeros_like(o_ref)
    carry_ref[...] = jnp.zeros_like(carry_ref)
    n_blocks = (qi + 1) * (tq // tk)

    def body(jj, _):
        j = n_blocks - 1 - jj
        start = pl.multiple_of(j * tk, tk)
        causal = (col + j * tk) < (row + qi * tq)
        causal = jnp.concatenate([causal] * heads, axis=0)
        kb = by_head(k_ref[pl.ds(start, tk), :]).astype(BF16)
        vb = by_head(v_ref[pl.ds(start, tk), :]).astype(BF16)
        z = lax.dot_general(q, kb, _BNT, preferred_element_type=F32)
        z = z.reshape(heads * tq, tk) * scale + bias
        w, rowsum = _sb_weights(z, causal, tri, carry_ref[...], F32)
        out = lax.dot_general(w.reshape(heads, tq, tk).astype(BF16), vb, _BNN,
                              preferred_element_type=F32)
        for h in range(heads):
            o_ref[:, h * d:(h + 1) * d] += out[h]
        carry_ref[...] += rowsum
        return 0

    lax.fori_loop(0, n_blocks, body, 0)


def sb_prompt_attention(qkv, sb_bias, batch, seq, tq=256, tk=128, heads=4):
    nq = seq // tq
    groups = SB_HEADS // heads
    width = heads * SB_HEAD_DIM
    kernel = functools.partial(_sb_prompt_kernel, tq=tq, tk=tk, heads=heads,
                               scale=SB_HEAD_DIM ** -0.5)
    return pl.pallas_call(
        kernel,
        grid=(batch, groups, nq),
        in_specs=[
            pl.BlockSpec(memory_space=pltpu.SMEM),
            pl.BlockSpec((tq, width), lambda b, g, i: (b * nq + i, g)),
            pl.BlockSpec((seq, width), lambda b, g, i: (b, groups + g)),
            pl.BlockSpec((seq, width), lambda b, g, i: (b, 2 * groups + g)),
        ],
        out_specs=pl.BlockSpec((tq, width), lambda b, g, i: (b * nq + i, g)),
        out_shape=jax.ShapeDtypeStruct((batch * seq, SB_WIDTH), F32),
        scratch_shapes=[pltpu.VMEM((heads * tq, 1), F32)],
        compiler_params=pltpu.CompilerParams(
            dimension_semantics=("parallel", "parallel", "arbitrary"),
            vmem_limit_bytes=VMEM_LIMIT),
        name="sb_prompt_attention",
    )(sb_bias, qkv, qkv, qkv)


SAMPLE_ROWS = 8


def _sb_sample_kernel(pt_ref, q_ref, bias_ref, *refs, n_steps, n_new, pages_per_step, scale):
    del pt_ref
    page_refs = refs[:2 * pages_per_step]
    kn_ref, vn_ref, o_ref, acc_ref, carry_ref, kpad_ref, vpad_ref = refs[2 * pages_per_step:]
    s = pl.program_id(1)
    d = SB_HEAD_DIM
    rr = SAMPLE_ROWS
    tri = _strict_upper_ones(PAGE_SIZE)
    q = q_ref[0]

    qb = q.reshape(SB_HEADS, rr, d)
    rows = SB_HEADS * rr

    def scores(k_heads, n_blocks):
        qq = qb if n_blocks == 1 else jnp.concatenate([qb] * n_blocks, axis=0)
        z = lax.dot_general(qq, k_heads, _BNT, preferred_element_type=F32)
        return z.reshape(n_blocks, rows, PAGE_SIZE) * scale + bias_ref[...]

    def values(w, v_heads, n_blocks):
        wb = w.reshape(n_blocks * SB_HEADS, rr, PAGE_SIZE).astype(BF16)
        out = lax.dot_general(wb, v_heads, _BNN, preferred_element_type=F32)
        return out.reshape(n_blocks, rows, d)

    def page_heads(refs):
        return jnp.stack([ref[pl.ds(h, PAGE_SIZE, stride=SB_HEADS), :].astype(BF16)
                          for ref in refs for h in range(SB_HEADS)])

    def pad_heads(ref):
        return jnp.stack([ref[:, h * d:(h + 1) * d].astype(BF16) for h in range(SB_HEADS)])

    @pl.when(s == 0)
    def _():
        kpad_ref[...] = jnp.zeros_like(kpad_ref)
        vpad_ref[...] = jnp.zeros_like(vpad_ref)
        kpad_ref[0:n_new, :] = kn_ref[0]
        vpad_ref[0:n_new, :] = vn_ref[0]
        row = lax.broadcasted_iota(jnp.int32, (rows, PAGE_SIZE), 0)
        col = lax.broadcasted_iota(jnp.int32, (rows, PAGE_SIZE), 1)
        causal = col < row % rr
        w, rowsum = _sb_weights(scores(pad_heads(kpad_ref), 1)[0], causal, tri, 0.0, BF16)
        acc_ref[...] = values(w, pad_heads(vpad_ref), 1)[0]
        carry_ref[...] = rowsum

    z = scores(page_heads(page_refs[0::2]), pages_per_step)
    carry = carry_ref[...]
    ws = []
    for i in range(pages_per_step):
        w, rowsum = _sb_weights(z[i], None, tri, carry, F32)
        ws.append(w)
        carry = carry + rowsum
    out = values(jnp.stack(ws), page_heads(page_refs[1::2]), pages_per_step)
    acc_ref[...] += jnp.sum(out, axis=0)
    carry_ref[...] = carry

    @pl.when(s == n_steps - 1)
    def _():
        for h in range(SB_HEADS):
            o_ref[0, :, h * d:(h + 1) * d] = acc_ref[h * rr:h * rr + n_new, :]


def sb_sample_attention(q, k_new, v_new, cache_k, cache_v, page_table, sb_bias):
    seqs, n_new, _ = q.shape
    n_pages = page_table.shape[1]
    pps = 4
    assert n_pages % pps == 0 and n_new <= SAMPLE_ROWS
    n_steps = n_pages // pps
    rows = SB_HEADS * SAMPLE_ROWS
    qh = jnp.swapaxes(q.reshape(seqs, n_new, SB_HEADS, SB_HEAD_DIM), 1, 2)
    qh = jnp.pad(qh, ((0, 0), (0, 0), (0, SAMPLE_ROWS - n_new), (0, 0)))
    qh = qh.reshape(seqs, rows, SB_HEAD_DIM).astype(BF16)
    bias = jnp.broadcast_to(jnp.repeat(sb_bias, SAMPLE_ROWS)[:, None], (rows, PAGE_SIZE)).astype(F32)

    def page_spec(offset):
        return pl.BlockSpec(
            (None, None, PAGE_SIZE * SB_HEADS, SB_HEAD_DIM),
            lambda b, s, pt: (0, pt[b, n_pages - 1 - offset - pps * s], 0, 0))

    new_spec = pl.BlockSpec((1, n_new, SB_WIDTH), lambda b, s, pt: (b, 0, 0))
    depth, n_pool = cache_k.shape[:2]
    cache_k = cache_k.reshape(depth, n_pool, PAGE_SIZE * SB_HEADS, SB_HEAD_DIM)
    cache_v = cache_v.reshape(depth, n_pool, PAGE_SIZE * SB_HEADS, SB_HEAD_DIM)
    kernel = functools.partial(_sb_sample_kernel, n_steps=n_steps, n_new=n_new,
                               pages_per_step=pps, scale=SB_HEAD_DIM ** -0.5)
    page_specs = [page_spec(i // 2) for i in range(2 * pps)]
    return pl.pallas_call(
        kernel,
        grid_spec=pltpu.PrefetchScalarGridSpec(
            num_scalar_prefetch=1,
            grid=(seqs, n_steps),
            in_specs=[
                pl.BlockSpec((1, rows, SB_HEAD_DIM), lambda b, s, pt: (b, 0, 0)),
                pl.BlockSpec((rows, PAGE_SIZE), lambda b, s, pt: (0, 0)),
                *page_specs, new_spec, new_spec,
            ],
            out_specs=new_spec,
            scratch_shapes=[
                pltpu.VMEM((rows, SB_HEAD_DIM), F32),
                pltpu.VMEM((rows, 1), F32),
                pltpu.VMEM((PAGE_SIZE, SB_WIDTH), F32),
                pltpu.VMEM((PAGE_SIZE, SB_WIDTH), F32),
            ],
        ),
        out_shape=jax.ShapeDtypeStruct((seqs, n_new, SB_WIDTH), F32),
        compiler_params=pltpu.CompilerParams(
            dimension_semantics=("parallel", "arbitrary"),
            vmem_limit_bytes=VMEM_LIMIT),
        name="sb_sample_attention",
    )(page_table, qh, bias, *([cache_k, cache_v] * pps), k_new, v_new)


_NN = (((1,), (0,)), ((), ()))
_NT = (((1,), (1,)), ((), ()))
_TN = (((0,), (0,)), ((), ()))
_BNN = (((2,), (1,)), ((0,), (0,)))
_BNT = (((2,), (2,)), ((0,), (0,)))
_BTN = (((1,), (1,)), ((0,), (0,)))


def _dot3(a, b, dims=_NN):
    ah, al = _split(a)
    bh, bl = _split(b)

    def d(x, y):
        return lax.dot_general(x, y, dims, preferred_element_type=F32)

    return d(ah, bh) + (d(al, bh) + d(ah, bl))


def _dot_ones_rhs(a, ones):
    n = a.shape[0]
    ah, al = _split(a)
    both = jnp.dot(jnp.concatenate([ah, al], axis=0), ones, preferred_element_type=F32)
    return both[:n] + both[n:]


def _dot_ones_lhs(ones, b):
    n = b.shape[1]
    bh, bl = _split(b)
    both = jnp.dot(ones, jnp.concatenate([bh, bl], axis=1), preferred_element_type=F32)
    return both[:, :n] + both[:, n:]


def _rwkv_chunk(r, k, v, lora_w, lora_a, g, prm, t0, *, chunk, t_valid):
    nb = r.shape[0]
    w0, a0, k_k, k_a, r_k, lnx_w, lnx_b = (prm[:, i:i + 1, :] for i in range(7))

    lane = lax.broadcasted_iota(jnp.int32, (1, 1, LANES), 2)
    head_masks = ((lane < RW_HEAD_DIM).astype(F32), (lane >= RW_HEAD_DIM).astype(F32))
    ri = lax.broadcasted_iota(jnp.int32, (LANES, LANES), 0)
    ci = lax.broadcasted_iota(jnp.int32, (LANES, LANES), 1)
    same_head = (ri // RW_HEAD_DIM) == (ci // RW_HEAD_DIM)
    same_head_ones = same_head.astype(BF16)
    tr = lax.broadcasted_iota(jnp.int32, (chunk, chunk), 0)
    tc = lax.broadcasted_iota(jnp.int32, (chunk, chunk), 1)
    strict = tr > tc
    incl = tr >= tc

    def head_sums(x):
        rows = x.shape[0] * x.shape[1]
        return _dot_ones_rhs(x.reshape(rows, LANES), same_head_ones).reshape(x.shape)

    def both_heads(x):
        return jnp.concatenate([x * m for m in head_masks], axis=0)

    def sum_heads(x):
        return x[:nb] + x[nb:]

    w_log = -_softplus(-(w0 + lora_w)) - 0.5
    log_decay = -jnp.exp(w_log)
    if t_valid < chunk:
        trow = lax.broadcasted_iota(jnp.int32, (1, chunk, 1), 1)
        log_decay = jnp.where(trow < t_valid, log_decay, 0.0)
    a = jax.nn.sigmoid(a0 + lora_a)
    kk = k * k_k
    k2 = k * (1.0 + (a - 1.0) * k_a)
    sums = head_sums(jnp.concatenate([kk * kk, r * k2 * r_k], axis=0))
    kk = kk / jnp.maximum(jnp.sqrt(sums[:nb]), 1e-12)
    bonus = sums[nb:] * v
    b = kk * a

    ld_hi, ld_lo = _split(log_decay)
    incl_b = jnp.broadcast_to(incl.astype(BF16)[None], (nb, chunk, chunk))
    cs2 = lax.dot_general(incl_b, jnp.concatenate([ld_hi, ld_lo], axis=2), _BNN,
                          preferred_element_type=F32)
    cs = cs2[:, :, :LANES] + cs2[:, :, LANES:]
    cs_last = cs[:, chunk - 1:chunk, :]
    at = -kk * jnp.exp(cs - log_decay)
    ginv = jnp.exp(-cs)
    bt = b * ginv
    kt = k2 * ginv
    rt = r * jnp.exp(cs)
    to_end = jnp.exp(cs_last - cs)

    eye_c = (tr == tc).astype(F32)
    decay_diag = jnp.where((ri == ci)[None], jnp.broadcast_to(jnp.exp(cs_last), (nb, LANES, LANES)), 0.0)
    from_state = _dot3(jnp.concatenate([at, rt, decay_diag], axis=1), t0, _BNN)
    rhs = from_state[:, :chunk]
    y = from_state[:, chunk:2 * chunk]

    keys = jnp.concatenate([bt, kt], axis=1)
    quad = _dot3(both_heads(jnp.concatenate([at, rt], axis=1)),
                 jnp.concatenate([keys, keys], axis=0), _BNT)
    vh = both_heads(v)
    a_ab = jnp.where(strict[None], quad[:, :chunk, :chunk], 0.0)
    a_rb = jnp.where(incl[None], quad[:, chunk:, :chunk], 0.0)
    a_ak = jnp.where(strict[None], quad[:, :chunk, chunk:], 0.0)
    a_rk = jnp.where(incl[None], quad[:, chunk:, chunk:], 0.0)
    rhs = rhs + sum_heads(_dot3(a_ak, vh, _BNN))
    y = y + sum_heads(_dot3(a_rk, vh, _BNN))

    power = a_ab
    inv = eye_c[None] + power
    span = 2
    while span < chunk:
        power = _dot3(power, power, _BNN)
        inv = inv + _dot3(inv, power, _BNN)
        span *= 2
    u = sum_heads(_dot3(inv, both_heads(rhs), _BNN))
    y = y + sum_heads(_dot3(a_rb, both_heads(u), _BNN))

    outer = _dot3(jnp.concatenate([b * to_end, k2 * to_end], axis=1),
                  jnp.concatenate([u, v], axis=1), _BTN)
    new_state = from_state[:, 2 * chunk:] + jnp.where(same_head[None], outer, 0.0)

    inv_n = 1.0 / RW_HEAD_DIM
    mu = head_sums(y) * inv_n
    d = y - mu
    var = head_sums(d * d) * inv_n
    yn = d * lax.rsqrt(var + GN_EPS) * lnx_w + lnx_b
    return (yn + bonus) * g, new_state


def _rwkv_kernel(r_ref, k_ref, v_ref, lw_ref, la_ref, g_ref, prm_ref, s0_ref, y_ref, st_ref,
                 state_ref, *, chunk, t_valid, n_chunks, pairs):
    c = pl.program_id(2)

    @pl.when(c == 0)
    def _():
        state_ref[...] = s0_ref[0]

    def by_pair(ref2d):
        return jnp.stack([ref2d[:, p * LANES:(p + 1) * LANES] for p in range(pairs)])

    out, new_state = _rwkv_chunk(
        by_pair(r_ref[0]), by_pair(k_ref[0]), by_pair(v_ref[0]), by_pair(lw_ref[0]),
        by_pair(la_ref[0]), by_pair(g_ref[0]), by_pair(prm_ref[...]), state_ref[...],
        chunk=chunk, t_valid=t_valid)
    for p in range(pairs):
        y_ref[0, :, p * LANES:(p + 1) * LANES] = out[p]
    state_ref[...] = new_state

    @pl.when(c == n_chunks - 1)
    def _():
        st_ref[0] = state_ref[...]


def rwkv7_scan(rkv, lora_w, lora_a, g, params, state_t, chunk, t_valid, pairs):
    batch, t, _ = g.shape
    n_chunks = t // chunk
    width = pairs * LANES
    groups = RW_PAIRS // pairs

    def col_spec(first_group):
        return pl.BlockSpec((1, chunk, width), lambda b, p, c: (b, c, first_group + p))

    seq_spec = col_spec(0)
    st_spec = pl.BlockSpec((1, pairs, LANES, LANES), lambda b, p, c: (b, p, 0, 0))
    kernel = functools.partial(_rwkv_kernel, chunk=chunk, t_valid=t_valid, n_chunks=n_chunks,
                               pairs=pairs)
    return pl.pallas_call(
        kernel,
        grid=(batch, groups, n_chunks),
        in_specs=[col_spec(0), col_spec(groups), col_spec(2 * groups)] + [seq_spec] * 3
        + [pl.BlockSpec((8, width), lambda b, p, c: (0, p)), st_spec],
        out_specs=[seq_spec, st_spec],
        out_shape=[jax.ShapeDtypeStruct((batch, t, RW_WIDTH), F32),
                   jax.ShapeDtypeStruct((batch, RW_PAIRS, LANES, LANES), F32)],
        scratch_shapes=[pltpu.VMEM((pairs, LANES, LANES), F32)],
        compiler_params=pltpu.CompilerParams(
            dimension_semantics=("parallel", "parallel", "arbitrary"),
            vmem_limit_bytes=VMEM_LIMIT),
        name="rwkv7_scan",
    )(rkv, rkv, rkv, lora_w, lora_a, g, params, state_t)


def _pack_state(s):
    batch = s.shape[0]
    st = jnp.swapaxes(s, -1, -2).reshape(batch, RW_PAIRS, 2, RW_HEAD_DIM, RW_HEAD_DIM)
    z = jnp.zeros_like(st[:, :, 0])
    top = jnp.concatenate([st[:, :, 0], z], axis=-1)
    bot = jnp.concatenate([z, st[:, :, 1]], axis=-1)
    return jnp.concatenate([top, bot], axis=-2)


def _unpack_state(t):
    batch = t.shape[0]
    d = RW_HEAD_DIM
    pair = jnp.stack([t[:, :, :d, :d], t[:, :, d:, d:]], axis=2)
    return jnp.swapaxes(pair.reshape(batch, RW_HEADS, d, d), -1, -2)


def rwkv7_branch(cols_rkv, cols_lora, shift_prev, wkv_prev, mu_rw, params, w2, a2, g2, chunk, t_pad,
                 pairs):
    batch, t, _ = cols_rkv.shape
    w = RW_WIDTH

    def token_shift(cols, prev0, mu):
        prev = jnp.concatenate([prev0[:, None], cols[:, :-1]], axis=1)
        return cols + (prev - cols) * mu

    rkv = token_shift(cols_rkv, shift_prev[:, :3 * w], mu_rw[:3 * w])
    mixed = token_shift(cols_lora, shift_prev[:, 3 * w:], mu_rw[3 * w:])
    wd = mixed[..., :DECAY_LORA]
    ad = mixed[..., DECAY_LORA:DECAY_LORA + ICLR_LORA]
    gd = mixed[..., DECAY_LORA + ICLR_LORA:]
    m = batch * t
    tm = min(512, m)
    lora_w = matmul(jnp.tanh(wd).reshape(m, -1).astype(BF16), w2, F32, tm=tm, tn=512, name="lora_w")
    lora_a = matmul(ad.reshape(m, -1).astype(BF16), a2, F32, tm=tm, tn=512, name="lora_a")
    g = matmul(jax.nn.sigmoid(gd).reshape(m, -1).astype(BF16), g2, F32, tm=tm, tn=512, name="lora_g")

    def seq(z):
        z = z.reshape(batch, t, -1)
        if t_pad > t:
            z = jnp.pad(z, ((0, 0), (0, t_pad - t), (0, 0)))
        return z

    y, st = rwkv7_scan(seq(rkv), seq(lora_w), seq(lora_a), seq(g), params,
                       _pack_state(wkv_prev), chunk, min(t, chunk), pairs)
    shift_new = jnp.concatenate([cols_rkv[:, -1], cols_lora[:, -1]], axis=-1)
    return y[:, :t].reshape(m, w), _unpack_state(st), shift_new


def kernel(x_prompt, x_sample, cache_k, cache_v, state_wkv, state_shift, page_table, ln1_g, w_in, sb_bias, mu_rw, w0, w2, a0, a2, g2, k_k, k_a, r_k, lnx_w, lnx_b, w_br_sb, w_br_rw, w_out, ln2_g, w_up, w_down, lnf_g):
    bp, seq, d_model = x_prompt.shape
    bs, dec, _ = x_sample.shape
    mp, ms = bp * seq, bs * dec
    tm = 512

    x = jnp.concatenate([x_prompt.reshape(mp, d_model), x_sample.reshape(ms, d_model)], axis=0)
    h = rms_norm(x, ln1_g[0], BF16)

    w_in0 = w_in[0]
    lora_in = DECAY_LORA + ICLR_LORA + GATE_LORA
    rkv_lo = 3 * SB_WIDTH
    lora_lo = rkv_lo + 3 * RW_WIDTH
    qkv_w = w_in0[:, :rkv_lo].astype(BF16)
    rkv_w = w_in0[:, rkv_lo:lora_lo].astype(BF16)
    lora_in_w = w_in0[:, lora_lo:lora_lo + lora_in].astype(BF16)
    gate_w = w_in0[:, lora_lo + lora_in:].astype(BF16)

    qkv = matmul(h, qkv_w, F32, tm=tm, tn=1024, name="proj_qkv")
    rkv_cols = matmul(h, rkv_w, F32, tm=tm, tn=1024, name="proj_rkv")
    lora_cols = matmul(h, lora_in_w, F32, tm=tm, tn=lora_in, name="proj_lora")
    gates = matmul(h, gate_w, F32, tm=tm, tn=1024, name="proj_gates")

    y_sb_p = sb_prompt_attention(qkv, sb_bias[0], bp, seq)
    qkv_s = qkv[mp:].reshape(bs, dec, 3 * SB_WIDTH)
    q_s, k_s, v_s = (qkv_s[..., i * SB_WIDTH:(i + 1) * SB_WIDTH] for i in range(3))
    y_sb_s = sb_sample_attention(
        q_s, k_s, v_s, cache_k, cache_v, page_table, sb_bias[0])
    y_sb = jnp.concatenate([y_sb_p, y_sb_s.reshape(ms, SB_WIDTH)], axis=0).astype(BF16)

    params = jnp.stack([w0[0], a0[0], k_k[0], k_a[0], r_k[0].reshape(-1), lnx_w[0], lnx_b[0],
                        jnp.zeros((RW_WIDTH,), F32)])
    w2b, a2b, g2b = w2[0].astype(BF16), a2[0].astype(BF16), g2[0].astype(BF16)
    y_rw_p, wkv_p, shift_p = rwkv7_branch(
        rkv_cols[:mp].reshape(bp, seq, -1), lora_cols[:mp].reshape(bp, seq, -1),
        jnp.zeros((bp, RW_PROJ), F32), jnp.zeros((bp, RW_HEADS, RW_HEAD_DIM, RW_HEAD_DIM), F32),
        mu_rw[0], params, w2b, a2b, g2b, chunk=128, t_pad=seq, pairs=4)
    y_rw_s, wkv_s, shift_s = rwkv7_branch(
        rkv_cols[mp:].reshape(bs, dec, -1), lora_cols[mp:].reshape(bs, dec, -1),
        state_shift[0], state_wkv[0], mu_rw[0], params, w2b, a2b, g2b, chunk=8, t_pad=8, pairs=8)
    y_rw = jnp.concatenate([y_rw_p, y_rw_s], axis=0).astype(BF16)

    tn = 1024
    m_sb = matmul(y_sb, w_br_sb[0].astype(BF16), F32, tm=tm, tn=tn, extras=(gates,),
                  epilogue=lambda acc, gt: jax.nn.sigmoid(gt) * acc, name="branch_sb")
    merged = matmul(y_rw, w_br_rw[0].astype(BF16), BF16, tm=tm, tn=tn, extras=(gates, m_sb),
                    extra_col_blocks=(d_model // tn, 0),
                    epilogue=lambda acc, gt, prev: prev + jax.nn.sigmoid(gt) * acc,
                    name="branch_rw")
    x1 = matmul(merged, w_out[0].astype(BF16), F32, tm=tm, tn=tn, extras=(x,),
                epilogue=lambda acc, res: res + acc, name="out_proj")
    h2 = rms_norm(x1, ln2_g[0], BF16)
    up = matmul(h2, w_up[0].astype(BF16), BF16, tm=tm, tn=tn,
                epilogue=lambda acc: jnp.square(jnp.maximum(acc, 0.0)), name="mlp_up")
    x2 = matmul(up, w_down[0].astype(BF16), F32, tm=tm, tn=tn, tk=2048, extras=(x1,),
                epilogue=lambda acc, res: res + acc, name="mlp_down")
    y = rms_norm(x2, lnf_g, F32)

    def kv_out(rows, lo, b, t):
        return rows[:, lo:lo + SB_WIDTH].reshape(1, b, t, SB_HEADS, SB_HEAD_DIM)

    return (y[:mp].reshape(bp, seq, d_model),
            y[mp:].reshape(bs, dec, d_model),
            kv_out(qkv[:mp], SB_WIDTH, bp, seq),
            kv_out(qkv[:mp], 2 * SB_WIDTH, bp, seq),
            wkv_p[None],
            shift_p[None],
            kv_out(qkv[mp:], SB_WIDTH, bs, dec),
            kv_out(qkv[mp:], 2 * SB_WIDTH, bs, dec),
            wkv_s[None],
            shift_s[None])
```

```python
import functools

import jax
import jax.numpy as jnp
from jax import lax
from jax.experimental import pallas as pl
from jax.experimental.pallas import tpu as pltpu

F32 = jnp.float32
BF16 = jnp.bfloat16
HIGHEST = lax.Precision.HIGHEST

LANES = 128
SB_HEADS = 8
SB_HEAD_DIM = 128
SB_WIDTH = SB_HEADS * SB_HEAD_DIM
RW_HEADS = 16
RW_HEAD_DIM = 64
RW_WIDTH = RW_HEADS * RW_HEAD_DIM
RW_PAIRS = RW_WIDTH // LANES
DECAY_LORA = 64
ICLR_LORA = 64
GATE_LORA = 160
RW_PROJ = 3 * RW_WIDTH + DECAY_LORA + ICLR_LORA + GATE_LORA
RMS_EPS = 1e-6
GN_EPS = 64e-5
PAGE_SIZE = 128
VMEM_LIMIT = 48 * 1024 * 1024


def _nt(a, b, precision=None):
    return lax.dot_general(a, b, (((1,), (1,)), ((), ())), precision=precision,
                           preferred_element_type=F32)


def _tn(a, b, precision=None):
    return lax.dot_general(a, b, (((0,), (0,)), ((), ())), precision=precision,
                           preferred_element_type=F32)


def _softplus(z):
    return jnp.maximum(z, 0.0) + jnp.log1p(jnp.exp(-jnp.abs(z)))


def _rms_kernel(x_ref, g_ref, o_ref):
    x = x_ref[...]
    y = x * lax.rsqrt(jnp.mean(x * x, axis=-1, keepdims=True) + RMS_EPS)
    o_ref[...] = (y * g_ref[...]).astype(o_ref.dtype)


def rms_norm(x, g, out_dtype, tm=256):
    m, d = x.shape
    return pl.pallas_call(
        _rms_kernel,
        grid=(m // tm,),
        in_specs=[pl.BlockSpec((tm, d), lambda i: (i, 0)),
                  pl.BlockSpec((1, d), lambda i: (0, 0))],
        out_specs=pl.BlockSpec((tm, d), lambda i: (i, 0)),
        out_shape=jax.ShapeDtypeStruct((m, d), out_dtype),
        compiler_params=pltpu.CompilerParams(dimension_semantics=("parallel",)),
        name="rms_norm",
    )(x, g.reshape(1, d))


def _mm_kernel(*refs, nk, n_extra, epilogue):
    a_ref, b_ref = refs[0], refs[1]
    extras = refs[2:2 + n_extra]
    o_ref = refs[2 + n_extra]

    def finish(acc):
        if epilogue is not None:
            acc = epilogue(acc, *[e[...] for e in extras])
        o_ref[...] = acc.astype(o_ref.dtype)

    if nk == 1:
        finish(jnp.dot(a_ref[...], b_ref[...], preferred_element_type=F32))
        return
    acc_ref = refs[3 + n_extra]
    k = pl.program_id(2)

    @pl.when(k == 0)
    def _():
        acc_ref[...] = jnp.zeros_like(acc_ref)

    acc_ref[...] += jnp.dot(a_ref[...], b_ref[...], preferred_element_type=F32)

    @pl.when(k == nk - 1)
    def _():
        finish(acc_ref[...])


def matmul(a, b, out_dtype, *, tm, tn, tk=None, extras=(), extra_col_blocks=None, epilogue=None,
           name="matmul"):
    m, kdim = a.shape
    _, n = b.shape
    tk = kdim if tk is None else tk
    nk = kdim // tk
    assert m % tm == 0 and n % tn == 0 and kdim % tk == 0
    in_specs = [pl.BlockSpec((tm, tk), lambda i, j, k: (i, k)),
                pl.BlockSpec((tk, tn), lambda i, j, k: (k, j))]
    offsets = extra_col_blocks or (0,) * len(extras)
    in_specs += [pl.BlockSpec((tm, tn), lambda i, j, k, off=off: (i, j + off)) for off in offsets]
    scratch = [pltpu.VMEM((tm, tn), F32)] if nk > 1 else []
    return pl.pallas_call(
        functools.partial(_mm_kernel, nk=nk, n_extra=len(extras), epilogue=epilogue),
        grid=(m // tm, n // tn, nk),
        in_specs=in_specs,
        out_specs=pl.BlockSpec((tm, tn), lambda i, j, k: (i, j)),
        out_shape=jax.ShapeDtypeStruct((m, n), out_dtype),
        scratch_shapes=scratch,
        compiler_params=pltpu.CompilerParams(
            dimension_semantics=("parallel", "parallel", "arbitrary"),
            vmem_limit_bytes=VMEM_LIMIT),
        name=name,
    )(a, b, *extras)


def _split(x):
    hi = x.astype(BF16)
    lo = (x - hi.astype(F32)).astype(BF16)
    return hi, lo


def _sb_weights(z, causal, tri, carry, out_dtype=BF16):
    n = z.shape[0]
    l1m = -_softplus(z)
    if causal is not None:
        l1m = jnp.where(causal, l1m, 0.0)
    hi, lo = _split(l1m)
    both = jnp.dot(jnp.concatenate([hi, lo], axis=0), tri, preferred_element_type=F32)
    between = both[:n] + both[n:]
    w = jnp.exp(z + l1m + between + carry)
    if causal is not None:
        w = jnp.where(causal, w, 0.0)
    return w.astype(out_dtype), jnp.sum(l1m, axis=1, keepdims=True)


def _strict_upper_ones(n):
    row = lax.broadcasted_iota(jnp.int32, (n, n), 0)
    col = lax.broadcasted_iota(jnp.int32, (n, n), 1)
    return (row > col).astype(BF16)


def _sb_prompt_kernel(bias_ref, q_ref, k_ref, v_ref, o_ref, *carry_refs, tq, tk, heads, scale):
    grp = pl.program_id(1)
    qi = pl.program_id(2)
    d = SB_HEAD_DIM
    row = lax.broadcasted_iota(jnp.int32, (tq, tk), 0)
    col = lax.broadcasted_iota(jnp.int32, (tq, tk), 1)
    tri = _strict_upper_ones(tk)
    (carry_ref,) = carry_refs

    def by_head(x):
        return jnp.stack([x[:, h * d:(h + 1) * d] for h in range(heads)])

    q = by_head(q_ref[...]).astype(BF16)
    bias = jnp.concatenate(
        [jnp.full((tq, 1), bias_ref[grp * heads + h], F32) for h in range(heads)], axis=0)
    o_ref[...] = jnp.zeros_like(o_ref)
    carry_ref[...] = jnp.zeros_like(carry_ref)
    n_blocks = (qi + 1) * (tq // tk)

    def body(jj, _):
        j = n_blocks - 1 - jj
        start = pl.multiple_of(j * tk, tk)
        causal = (col + j * tk) < (row + qi * tq)
        causal = jnp.concatenate([causal] * heads, axis=0)
        kb = by_head(k_ref[pl.ds(start, tk), :]).astype(BF16)
        vb = by_head(v_ref[pl.ds(start, tk), :]).astype(BF16)
        z = lax.dot_general(q, kb, _BNT, preferred_element_type=F32)
        z = z.reshape(heads * tq, tk) * scale + bias
        w, rowsum = _sb_weights(z, causal, tri, carry_ref[...], F32)
        out = lax.dot_general(w.reshape(heads, tq, tk).astype(BF16), vb, _BNN,
                              preferred_element_type=F32)
        for h in range(heads):
            o_ref[:, h * d:(h + 1) * d] += out[h]
        carry_ref[...] += rowsum
        return 0

    lax.fori_loop(0, n_blocks, body, 0)


def sb_prompt_attention(qkv, sb_bias, batch, seq, tq=256, tk=128, heads=4):
    nq = seq // tq
    groups = SB_HEADS // heads
    width = heads * SB_HEAD_DIM
    kernel = functools.partial(_sb_prompt_kernel, tq=tq, tk=tk, heads=heads,
                               scale=SB_HEAD_DIM ** -0.5)
    return pl.pallas_call(
        kernel,
        grid=(batch, groups, nq),
        in_specs=[
            pl.BlockSpec(memory_space=pltpu.SMEM),
            pl.BlockSpec((tq, width), lambda b, g, i: (b * nq + i, g)),
            pl.BlockSpec((seq, width), lambda b, g, i: (b, groups + g)),
            pl.BlockSpec((seq, width), lambda b, g, i: (b, 2 * groups + g)),
        ],
        out_specs=pl.BlockSpec((tq, width), lambda b, g, i: (b * nq + i, g)),
        out_shape=jax.ShapeDtypeStruct((batch * seq, SB_WIDTH), F32),
        scratch_shapes=[pltpu.VMEM((heads * tq, 1), F32)],
        compiler_params=pltpu.CompilerParams(
            dimension_semantics=("parallel", "parallel", "arbitrary"),
            vmem_limit_bytes=VMEM_LIMIT),
        name="sb_prompt_attention",
    )(sb_bias, qkv, qkv, qkv)


SAMPLE_ROWS = 8


def _sb_sample_kernel(pt_ref, q_ref, bias_ref, *refs, n_steps, n_new, pages_per_step, scale):
    del pt_ref
    page_refs = refs[:2 * pages_per_step]
    kn_ref, vn_ref, o_ref, acc_ref, carry_ref, kpad_ref, vpad_ref = refs[2 * pages_per_step:]
    s = pl.program_id(1)
    d = SB_HEAD_DIM
    rr = SAMPLE_ROWS
    tri = _strict_upper_ones(PAGE_SIZE)
    q = q_ref[0]

    qb = q.reshape(SB_HEADS, rr, d)
    rows = SB_HEADS * rr

    def scores(k_heads, n_blocks):
        qq = qb if n_blocks == 1 else jnp.concatenate([qb] * n_blocks, axis=0)
        z = lax.dot_general(qq, k_heads, _BNT, preferred_element_type=F32)
        return z.reshape(n_blocks, rows, PAGE_SIZE) * scale + bias_ref[...]

    def values(w, v_heads, n_blocks):
        wb = w.reshape(n_blocks * SB_HEADS, rr, PAGE_SIZE).astype(BF16)
        out = lax.dot_general(wb, v_heads, _BNN, preferred_element_type=F32)
        return out.reshape(n_blocks, rows, d)

    def page_heads(refs):
        return jnp.stack([ref[pl.ds(h, PAGE_SIZE, stride=SB_HEADS), :].astype(BF16)
                          for ref in refs for h in range(SB_HEADS)])

    def pad_heads(ref):
        return jnp.stack([ref[:, h * d:(h + 1) * d].astype(BF16) for h in range(SB_HEADS)])

    @pl.when(s == 0)
    def _():
        kpad_ref[...] = jnp.zeros_like(kpad_ref)
        vpad_ref[...] = jnp.zeros_like(vpad_ref)
        kpad_ref[0:n_new, :] = kn_ref[0]
        vpad_ref[0:n_new, :] = vn_ref[0]
        row = lax.broadcasted_iota(jnp.int32, (rows, PAGE_SIZE), 0)
        col = lax.broadcasted_iota(jnp.int32, (rows, PAGE_SIZE), 1)
        causal = col < row % rr
        w, rowsum = _sb_weights(scores(pad_heads(kpad_ref), 1)[0], causal, tri, 0.0, BF16)
        acc_ref[...] = values(w, pad_heads(vpad_ref), 1)[0]
        carry_ref[...] = rowsum

    z = scores(page_heads(page_refs[0::2]), pages_per_step)
    carry = carry_ref[...]
    ws = []
    for i in range(pages_per_step):
        w, rowsum = _sb_weights(z[i], None, tri, carry, F32)
        ws.append(w)
        carry = carry + rowsum
    out = values(jnp.stack(ws), page_heads(page_refs[1::2]), pages_per_step)
    acc_ref[...] += jnp.sum(out, axis=0)
    carry_ref[...] = carry

    @pl.when(s == n_steps - 1)
    def _():
        for h in range(SB_HEADS):
            o_ref[0, :, h * d:(h + 1) * d] = acc_ref[h * rr:h * rr + n_new, :]


def sb_sample_attention(q, k_new, v_new, cache_k, cache_v, page_table, sb_bias):
    seqs, n_new, _ = q.shape
    n_pages = page_table.shape[1]
    pps = 4
    assert n_pages % pps == 0 and n_new <= SAMPLE_ROWS
    n_steps = n_pages // pps
    rows = SB_HEADS * SAMPLE_ROWS
    qh = jnp.swapaxes(q.reshape(seqs, n_new, SB_HEADS, SB_HEAD_DIM), 1, 2)
    qh = jnp.pad(qh, ((0, 0), (0, 0), (0, SAMPLE_ROWS - n_new), (0, 0)))
    qh = qh.reshape(seqs, rows, SB_HEAD_DIM).astype(BF16)
    bias = jnp.broadcast_to(jnp.repeat(sb_bias, SAMPLE_ROWS)[:, None], (rows, PAGE_SIZE)).astype(F32)

    def page_spec(offset):
        return pl.BlockSpec(
            (None, None, PAGE_SIZE * SB_HEADS, SB_HEAD_DIM),
            lambda b, s, pt: (0, pt[b, n_pages - 1 - offset - pps * s], 0, 0))

    new_spec = pl.BlockSpec((1, n_new, SB_WIDTH), lambda b, s, pt: (b, 0, 0))
    depth, n_pool = cache_k.shape[:2]
    cache_k = cache_k.reshape(depth, n_pool, PAGE_SIZE * SB_HEADS, SB_HEAD_DIM)
    cache_v = cache_v.reshape(depth, n_pool, PAGE_SIZE * SB_HEADS, SB_HEAD_DIM)
    kernel = functools.partial(_sb_sample_kernel, n_steps=n_steps, n_new=n_new,
                               pages_per_step=pps, scale=SB_HEAD_DIM ** -0.5)
    page_specs = [page_spec(i // 2) for i in range(2 * pps)]
    return pl.pallas_call(
        kernel,
        grid_spec=pltpu.PrefetchScalarGridSpec(
            num_scalar_prefetch=1,
            grid=(seqs, n_steps),
            in_specs=[
                pl.BlockSpec((1, rows, SB_HEAD_DIM), lambda b, s, pt: (b, 0, 0)),
                pl.BlockSpec((rows, PAGE_SIZE), lambda b, s, pt: (0, 0)),
                *page_specs, new_spec, new_spec,
            ],
            out_specs=new_spec,
            scratch_shapes=[
                pltpu.VMEM((rows, SB_HEAD_DIM), F32),
                pltpu.VMEM((rows, 1), F32),
                pltpu.VMEM((PAGE_SIZE, SB_WIDTH), F32),
                pltpu.VMEM((PAGE_SIZE, SB_WIDTH), F32),
            ],
        ),
        out_shape=jax.ShapeDtypeStruct((seqs, n_new, SB_WIDTH), F32),
        compiler_params=pltpu.CompilerParams(
            dimension_semantics=("parallel", "arbitrary"),
            vmem_limit_bytes=VMEM_LIMIT),
        name="sb_sample_attention",
    )(page_table, qh, bias, *([cache_k, cache_v] * pps), k_new, v_new)


_NN = (((1,), (0,)), ((), ()))
_NT = (((1,), (1,)), ((), ()))
_TN = (((0,), (0,)), ((), ()))
_BNN = (((2,), (1,)), ((0,), (0,)))
_BNT = (((2,), (2,)), ((0,), (0,)))
_BTN = (((1,), (1,)), ((0,), (0,)))


def _dot3(a, b, dims=_NN):
    ah, al = _split(a)
    bh, bl = _split(b)

    def d(x, y):
        return lax.dot_general(x, y, dims, preferred_element_type=F32)

    return d(ah, bh) + (d(al, bh) + d(ah, bl))


def _dot_ones_rhs(a, ones):
    n = a.shape[0]
    ah, al = _split(a)
    both = jnp.dot(jnp.concatenate([ah, al], axis=0), ones, preferred_element_type=F32)
    return both[:n] + both[n:]


def _dot_ones_lhs(ones, b):
    n = b.shape[1]
    bh, bl = _split(b)
    both = jnp.dot(ones, jnp.concatenate([bh, bl], axis=1), preferred_element_type=F32)
    return both[:, :n] + both[:, n:]


def _rwkv_chunk(r, k, v, lora_w, lora_a, g, prm, t0, *, chunk, t_valid):
    nb = r.shape[0]
    w0, a0, k_k, k_a, r_k, lnx_w, lnx_b = (prm[:, i:i + 1, :] for i in range(7))

    lane = lax.broadcasted_iota(jnp.int32, (1, 1, LANES), 2)
    head_masks = ((lane < RW_HEAD_DIM).astype(F32), (lane >= RW_HEAD_DIM).astype(F32))
    ri = lax.broadcasted_iota(jnp.int32, (LANES, LANES), 0)
    ci = lax.broadcasted_iota(jnp.int32, (LANES, LANES), 1)
    same_head = (ri // RW_HEAD_DIM) == (ci // RW_HEAD_DIM)
    same_head_ones = same_head.astype(BF16)
    tr = lax.broadcasted_iota(jnp.int32, (chunk, chunk), 0)
    tc = lax.broadcasted_iota(jnp.int32, (chunk, chunk), 1)
    strict = tr > tc
    incl = tr >= tc

    def head_sums(x):
        rows = x.shape[0] * x.shape[1]
        return _dot_ones_rhs(x.reshape(rows, LANES), same_head_ones).reshape(x.shape)

    def both_heads(x):
        return jnp.concatenate([x * m for m in head_masks], axis=0)

    def sum_heads(x):
        return x[:nb] + x[nb:]

    w_log = -_softplus(-(w0 + lora_w)) - 0.5
    log_decay = -jnp.exp(w_log)
    if t_valid < chunk:
        trow = lax.broadcasted_iota(jnp.int32, (1, chunk, 1), 1)
        log_decay = jnp.where(trow < t_valid, log_decay, 0.0)
    a = jax.nn.sigmoid(a0 + lora_a)
    kk = k * k_k
    k2 = k * (1.0 + (a - 1.0) * k_a)
    sums = head_sums(jnp.concatenate([kk * kk, r * k2 * r_k], axis=0))
    kk = kk / jnp.maximum(jnp.sqrt(sums[:nb]), 1e-12)
    bonus = sums[nb:] * v
    b = kk * a

    ld_hi, ld_lo = _split(log_decay)
    incl_b = jnp.broadcast_to(incl.astype(BF16)[None], (nb, chunk, chunk))
    cs2 = lax.dot_general(incl_b, jnp.concatenate([ld_hi, ld_lo], axis=2), _BNN,
                          preferred_element_type=F32)
    cs = cs2[:, :, :LANES] + cs2[:, :, LANES:]
    cs_last = cs[:, chunk - 1:chunk, :]
    at = -kk * jnp.exp(cs - log_decay)
    ginv = jnp.exp(-cs)
    bt = b * ginv
    kt = k2 * ginv
    rt = r * jnp.exp(cs)
    to_end = jnp.exp(cs_last - cs)

    eye_c = (tr == tc).astype(F32)
    decay_diag = jnp.where((ri == ci)[None], jnp.broadcast_to(jnp.exp(cs_last), (nb, LANES, LANES)), 0.0)
    from_state = _dot3(jnp.concatenate([at, rt, decay_diag], axis=1), t0, _BNN)
    rhs = from_state[:, :chunk]
    y = from_state[:, chunk:2 * chunk]

    keys = jnp.concatenate([bt, kt], axis=1)
    quad = _dot3(both_heads(jnp.concatenate([at, rt], axis=1)),
                 jnp.concatenate([keys, keys], axis=0), _BNT)
    vh = both_heads(v)
    a_ab = jnp.where(strict[None], quad[:, :chunk, :chunk], 0.0)
    a_rb = jnp.where(incl[None], quad[:, chunk:, :chunk], 0.0)
    a_ak = jnp.where(strict[None], quad[:, :chunk, chunk:], 0.0)
    a_rk = jnp.where(incl[None], quad[:, chunk:, chunk:], 0.0)
    rhs = rhs + sum_heads(_dot3(a_ak, vh, _BNN))
    y = y + sum_heads(_dot3(a_rk, vh, _BNN))

    power = a_ab
    inv = eye_c[None] + power
    span = 2
    while span < chunk:
        power = _dot3(power, power, _BNN)
        inv = inv + _dot3(inv, power, _BNN)
        span *= 2
    u = sum_heads(_dot3(inv, both_heads(rhs), _BNN))
    y = y + sum_heads(_dot3(a_rb, both_heads(u), _BNN))

    outer = _dot3(jnp.concatenate([b * to_end, k2 * to_end], axis=1),
                  jnp.concatenate([u, v], axis=1), _BTN)
    new_state = from_state[:, 2 * chunk:] + jnp.where(same_head[None], outer, 0.0)

    inv_n = 1.0 / RW_HEAD_DIM
    mu = head_sums(y) * inv_n
    d = y - mu
    var = head_sums(d * d) * inv_n
    yn = d * lax.rsqrt(var + GN_EPS) * lnx_w + lnx_b
    return (yn + bonus) * g, new_state


def _rwkv_kernel(r_ref, k_ref, v_ref, lw_ref, la_ref, g_ref, prm_ref, s0_ref, y_ref, st_ref,
                 state_ref, *, chunk, t_valid, n_chunks, pairs):
    c = pl.program_id(2)

    @pl.when(c == 0)
    def _():
        state_ref[...] = s0_ref[0]

    def by_pair(ref2d):
        return jnp.stack([ref2d[:, p * LANES:(p + 1) * LANES] for p in range(pairs)])

    out, new_state = _rwkv_chunk(
        by_pair(r_ref[0]), by_pair(k_ref[0]), by_pair(v_ref[0]), by_pair(lw_ref[0]),
        by_pair(la_ref[0]), by_pair(g_ref[0]), by_pair(prm_ref[...]), state_ref[...],
        chunk=chunk, t_valid=t_valid)
    for p in range(pairs):
        y_ref[0, :, p * LANES:(p + 1) * LANES] = out[p]
    state_ref[...] = new_state

    @pl.when(c == n_chunks - 1)
    def _():
        st_ref[0] = state_ref[...]


def rwkv7_scan(rkv, lora_w, lora_a, g, params, state_t, batch, n_chunks, t_valid, pairs):
    chunk = g.shape[1]
    width = pairs * LANES
    groups = RW_PAIRS // pairs

    def col_spec(first_group):
        return pl.BlockSpec((1, chunk, width),
                            lambda b, p, c: (b * n_chunks + c, 0, first_group + p))

    seq_spec = col_spec(0)
    st_spec = pl.BlockSpec((1, pairs, LANES, LANES), lambda b, p, c: (b, p, 0, 0))
    kernel = functools.partial(_rwkv_kernel, chunk=chunk, t_valid=t_valid, n_chunks=n_chunks,
                               pairs=pairs)
    return pl.pallas_call(
        kernel,
        grid=(batch, groups, n_chunks),
        in_specs=[col_spec(0), col_spec(groups), col_spec(2 * groups)] + [seq_spec] * 3
        + [pl.BlockSpec((8, width), lambda b, p, c: (0, p)), st_spec],
        out_specs=[seq_spec, st_spec],
        out_shape=[jax.ShapeDtypeStruct((batch * n_chunks, chunk, RW_WIDTH), F32),
                   jax.ShapeDtypeStruct((batch, RW_PAIRS, LANES, LANES), F32)],
        scratch_shapes=[pltpu.VMEM((pairs, LANES, LANES), F32)],
        compiler_params=pltpu.CompilerParams(
            dimension_semantics=("parallel", "parallel", "arbitrary"),
            vmem_limit_bytes=VMEM_LIMIT),
        name="rwkv7_scan",
    )(rkv, rkv, rkv, lora_w, lora_a, g, params, state_t)


def _pack_state(s):
    batch = s.shape[0]
    st = jnp.swapaxes(s, -1, -2).reshape(batch, RW_PAIRS, 2, RW_HEAD_DIM, RW_HEAD_DIM)
    z = jnp.zeros_like(st[:, :, 0])
    top = jnp.concatenate([st[:, :, 0], z], axis=-1)
    bot = jnp.concatenate([z, st[:, :, 1]], axis=-1)
    return jnp.concatenate([top, bot], axis=-2)


def _unpack_state(t):
    batch = t.shape[0]
    d = RW_HEAD_DIM
    pair = jnp.stack([t[:, :, :d, :d], t[:, :, d:, d:]], axis=2)
    return jnp.swapaxes(pair.reshape(batch, RW_HEADS, d, d), -1, -2)


PROMPT_CHUNK = 128
SAMPLE_CHUNK = 8


def _token_shift(cols, mu, prev_sample, mp, seq, dec):
    m, width = cols.shape
    shifted = jnp.concatenate([jnp.zeros((1, width), cols.dtype), cols[:-1]], axis=0)
    row = lax.broadcasted_iota(jnp.int32, (m, 1), 0)
    prompt_start = (row < mp) & (row % seq == 0)
    sample_start = (row >= mp) & ((row - mp) % dec == 0)
    sample_prev = jnp.concatenate(
        [jnp.zeros((mp, width), cols.dtype), jnp.repeat(prev_sample, dec, axis=0)], axis=0)
    prev = jnp.where(prompt_start, 0.0, jnp.where(sample_start, sample_prev, shifted))
    return cols + (prev - cols) * mu


def rwkv7_mixer(cols_rkv, cols_lora, shift_prev, wkv_prev, mu_rw, params, w2, a2, g2, bp, seq, bs, dec):
    m = cols_rkv.shape[0]
    mp = bp * seq
    w = RW_WIDTH
    rkv = _token_shift(cols_rkv, mu_rw[:3 * w], shift_prev[:, :3 * w], mp, seq, dec)
    mixed = _token_shift(cols_lora, mu_rw[3 * w:], shift_prev[:, 3 * w:], mp, seq, dec)
    wd = mixed[:, :DECAY_LORA]
    ad = mixed[:, DECAY_LORA:DECAY_LORA + ICLR_LORA]
    gd = mixed[:, DECAY_LORA + ICLR_LORA:]
    tm = min(512, m)
    lora_w = matmul(jnp.tanh(wd).astype(BF16), w2, F32, tm=tm, tn=w, name="lora_w")
    lora_a = matmul(ad.astype(BF16), a2, F32, tm=tm, tn=w, name="lora_a")
    g = matmul(jax.nn.sigmoid(gd).astype(BF16), g2, F32, tm=tm, tn=w, name="lora_g")
    seqs = (rkv, lora_w, lora_a, g)

    assert seq % PROMPT_CHUNK == 0 and m % PROMPT_CHUNK == 0
    y_p, st_p = rwkv7_scan(
        *(z.reshape(m // PROMPT_CHUNK, PROMPT_CHUNK, -1) for z in seqs), params,
        jnp.zeros((bp, RW_PAIRS, LANES, LANES), F32), bp, seq // PROMPT_CHUNK, PROMPT_CHUNK,
        pairs=4)

    def sample_blocks(z):
        z = z[mp:].reshape(bs, dec, -1)
        return jnp.pad(z, ((0, 0), (0, SAMPLE_CHUNK - dec), (0, 0)))

    y_s, st_s = rwkv7_scan(*(sample_blocks(z) for z in seqs), params, _pack_state(wkv_prev),
                           bs, 1, dec, pairs=8)
    y = jnp.concatenate([y_p.reshape(mp, w), y_s[:, :dec].reshape(bs * dec, w)], axis=0)

    def last_rows(lo, n_seq, length):
        idx = lo + length - 1 + length * jnp.arange(n_seq)
        return jnp.concatenate([cols_rkv[idx], cols_lora[idx]], axis=-1)

    return (y, _unpack_state(st_p), _unpack_state(st_s),
            last_rows(0, bp, seq), last_rows(mp, bs, dec))


def kernel(x_prompt, x_sample, cache_k, cache_v, state_wkv, state_shift, page_table, ln1_g, w_in, sb_bias, mu_rw, w0, w2, a0, a2, g2, k_k, k_a, r_k, lnx_w, lnx_b, w_br_sb, w_br_rw, w_out, ln2_g, w_up, w_down, lnf_g):
    bp, seq, d_model = x_prompt.shape
    bs, dec, _ = x_sample.shape
    mp, ms = bp * seq, bs * dec
    tm = 512

    x = jnp.concatenate([x_prompt.reshape(mp, d_model), x_sample.reshape(ms, d_model)], axis=0)
    h = rms_norm(x, ln1_g[0], BF16)

    w_in0 = w_in[0]
    lora_in = DECAY_LORA + ICLR_LORA + GATE_LORA
    rkv_lo = 3 * SB_WIDTH
    lora_lo = rkv_lo + 3 * RW_WIDTH
    qkv_w = w_in0[:, :rkv_lo].astype(BF16)
    rkv_w = w_in0[:, rkv_lo:lora_lo].astype(BF16)
    lora_in_w = w_in0[:, lora_lo:lora_lo + lora_in].astype(BF16)
    gate_w = w_in0[:, lora_lo + lora_in:].astype(BF16)

    qkv = matmul(h, qkv_w, F32, tm=tm, tn=1024, name="proj_qkv")
    rkv_cols = matmul(h, rkv_w, F32, tm=tm, tn=1024, name="proj_rkv")
    lora_cols = matmul(h, lora_in_w, F32, tm=tm, tn=lora_in, name="proj_lora")
    gates = matmul(h, gate_w, F32, tm=tm, tn=1024, name="proj_gates")

    y_sb_p = sb_prompt_attention(qkv, sb_bias[0], bp, seq)
    qkv_s = qkv[mp:].reshape(bs, dec, 3 * SB_WIDTH)
    q_s, k_s, v_s = (qkv_s[..., i * SB_WIDTH:(i + 1) * SB_WIDTH] for i in range(3))
    y_sb_s = sb_sample_attention(
        q_s, k_s, v_s, cache_k, cache_v, page_table, sb_bias[0])
    y_sb = jnp.concatenate([y_sb_p, y_sb_s.reshape(ms, SB_WIDTH)], axis=0).astype(BF16)

    params = jnp.stack([w0[0], a0[0], k_k[0], k_a[0], r_k[0].reshape(-1), lnx_w[0], lnx_b[0],
                        jnp.zeros((RW_WIDTH,), F32)])
    w2b, a2b, g2b = w2[0].astype(BF16), a2[0].astype(BF16), g2[0].astype(BF16)
    y_rw, wkv_p, wkv_s, shift_p, shift_s = rwkv7_mixer(
        rkv_cols, lora_cols, state_shift[0], state_wkv[0], mu_rw[0], params, w2b, a2b, g2b,
        bp, seq, bs, dec)
    y_rw = y_rw.astype(BF16)

    tn = 1024
    m_sb = matmul(y_sb, w_br_sb[0].astype(BF16), F32, tm=tm, tn=tn, extras=(gates,),
                  epilogue=lambda acc, gt: jax.nn.sigmoid(gt) * acc, name="branch_sb")
    merged = matmul(y_rw, w_br_rw[0].astype(BF16), BF16, tm=tm, tn=tn, extras=(gates, m_sb),
                    extra_col_blocks=(d_model // tn, 0),
                    epilogue=lambda acc, gt, prev: prev + jax.nn.sigmoid(gt) * acc,
                    name="branch_rw")
    x1 = matmul(merged, w_out[0].astype(BF16), F32, tm=tm, tn=tn, extras=(x,),
                epilogue=lambda acc, res: res + acc, name="out_proj")
    h2 = rms_norm(x1, ln2_g[0], BF16)
    up = matmul(h2, w_up[0].astype(BF16), BF16, tm=tm, tn=tn,
                epilogue=lambda acc: jnp.square(jnp.maximum(acc, 0.0)), name="mlp_up")
    x2 = matmul(up, w_down[0].astype(BF16), F32, tm=tm, tn=tn, tk=2048, extras=(x1,),
                epilogue=lambda acc, res: res + acc, name="mlp_down")
    y = rms_norm(x2, lnf_g, F32)

    def kv_out(rows, lo, b, t):
        return rows[:, lo:lo + SB_WIDTH].reshape(1, b, t, SB_HEADS, SB_HEAD_DIM)

    return (y[:mp].reshape(bp, seq, d_model),
            y[mp:].reshape(bs, dec, d_model),
            kv_out(qkv[:mp], SB_WIDTH, bp, seq),
            kv_out(qkv[:mp], 2 * SB_WIDTH, bp, seq),
            wkv_p[None],
            shift_p[None],
            kv_out(qkv[mp:], SB_WIDTH, bs, dec),
            kv_out(qkv[mp:], 2 * SB_WIDTH, bs, dec),
            wkv_s[None],
            shift_s[None])
```
